```python
import functools
import jax, jax.numpy as jnp
from jax import lax
import numpy as np

D_MODEL = 1024
BATCH = 8
SEQ = 2048
DEPTH = 4
DEC_BATCH = 128
DEC_SEQ = 4
PAST_LEN = 2048
PAGE_SIZE = 128

N_EVEN = (DEPTH + 1) // 2
N_ODD = DEPTH // 2
FOX_HEADS = 8
HEAD_DIM = 64
FOX_W = FOX_HEADS * HEAD_DIM
RWKV_HEADS = 8
RWKV_N = 64
RWKV_W = RWKV_HEADS * RWKV_N
W_LORA = 64
A_LORA = 64
G_LORA = 128
FOX_IN = 3 * FOX_W + FOX_HEADS
RWKV_IN = 3 * RWKV_W + W_LORA + A_LORA + G_LORA
IN_W = FOX_IN + RWKV_IN
MIX_W = FOX_W + RWKV_W
Q_BLOCK = 128
POOL_WINDOWS = (2, 4, 8, 16)
POOL_GROUPS = len(POOL_WINDOWS)
POOL_C = D_MODEL // POOL_GROUPS
POOL_BUF = max(POOL_WINDOWS) - 1
D_FF = 128 * ((8 * D_MODEL // 3 + 127) // 128)
N_EXPERTS = 8
TOP_K = 2
D_FF_EXPERT = 7 * D_MODEL // 2
RMS_EPS = 1e-6
GN_EPS = 64e-5
NEG_INF = -1e30

kernel_name = 'fox_rwkv7_pool_moe_decoder_step'


def rms_norm(x, g):
    xf = x.astype(jnp.float32)
    y = xf * lax.rsqrt(jnp.mean(xf * xf, axis=-1, keepdims=True) + RMS_EPS)
    return (y * g.astype(jnp.float32)).astype(x.dtype)


def ada_params(c, w, b):
    mod = jnp.einsum('bd,de->be', jax.nn.silu(c), w) + b
    return jnp.split(mod[:, None, :], 6, axis=-1)


def modulate(x, g, shift, scale):
    return rms_norm(x, g) * (1 + scale) + shift


def fox_prompt_attention(q, k, v, logf):
    B, T, H, Dh = q.shape
    scale = Dh ** -0.5
    FT = jnp.swapaxes(jnp.cumsum(logf, axis=1), 1, 2)
    nb = T // Q_BLOCK
    qb = jnp.moveaxis(q.reshape(B, nb, Q_BLOCK, H, Dh), 1, 0)
    Fb = jnp.moveaxis(FT.reshape(B, H, nb, Q_BLOCK), 2, 0)
    kpos = jnp.arange(T)

    def block(args):
        i, qi, Fi = args
        s = jnp.einsum('bqhd,bkhd->bhqk', qi, k).astype(jnp.float32) * scale
        s = s + Fi[..., :, None] - FT[..., None, :]
        qpos = i * Q_BLOCK + jnp.arange(Q_BLOCK)
        s = jnp.where(kpos[None, :] <= qpos[:, None], s, NEG_INF)
        p = jax.nn.softmax(s, axis=-1).astype(v.dtype)
        return jnp.einsum('bhqk,bkhd->bqhd', p, v)

    o = lax.map(block, (jnp.arange(nb), qb, Fb))
    return jnp.moveaxis(o, 0, 1).reshape(B, T, H * Dh)


def fox_sample_attention(q, k, v, logf, k_past, v_past, logf_past):
    B, T, H, Dh = q.shape
    P = k_past.shape[1]
    scale = Dh ** -0.5
    cs = jnp.cumsum(logf_past.astype(jnp.float32), axis=1)
    suffix = jnp.swapaxes(cs[:, -1:] - cs, 1, 2)
    Fn = jnp.swapaxes(jnp.cumsum(logf, axis=1), 1, 2)
    s_past = jnp.einsum('bqhd,bkhd->bhqk', q, k_past.astype(q.dtype)).astype(jnp.float32) * scale
    s_past = s_past + Fn[..., :, None] + suffix[..., None, :]
    s_new = jnp.einsum('bqhd,bkhd->bhqk', q, k).astype(jnp.float32) * scale
    s_new = s_new + Fn[..., :, None] - Fn[..., None, :]
    s_new = jnp.where(jnp.tril(jnp.ones((T, T), bool)), s_new, NEG_INF)
    p = jax.nn.softmax(jnp.concatenate([s_past, s_new], axis=-1), axis=-1).astype(v.dtype)
    o = jnp.einsum('bhqk,bkhd->bqhd', p[..., :P], v_past.astype(v.dtype))
    o = o + jnp.einsum('bhqk,bkhd->bqhd', p[..., P:], v)
    return o.reshape(B, T, H * Dh)


def rwkv7_time_mix(p, prev, S0, mu, w0, w2, a0, a2, g2, k_k, k_a, r_k, ln_w, ln_b):
    B, T, _ = p.shape
    f32 = jnp.float32
    shifted = jnp.concatenate([prev[:, None, :].astype(p.dtype), p[:, :-1]], axis=1)
    xm = p + (shifted - p) * mu
    cuts = np.cumsum([RWKV_W, W_LORA, RWKV_W, RWKV_W, A_LORA]).tolist()
    r, wl, k, v, al, gl = jnp.split(xm, cuts, axis=-1)
    w = -jax.nn.softplus(-(w0 + jnp.tanh(wl) @ w2).astype(f32)) - 0.5
    decay = jnp.exp(-jnp.exp(w))
    a = jax.nn.sigmoid((a0 + al @ a2).astype(f32))
    g = jax.nn.sigmoid(gl) @ g2
    heads = lambda t: t.reshape(B, T, RWKV_HEADS, RWKV_N)
    kk = heads((k * k_k).astype(f32))
    kk = kk * lax.rsqrt(jnp.maximum(jnp.sum(kk * kk, axis=-1, keepdims=True), 1e-24))
    k_mod = k.astype(f32) * (1 + (a - 1) * k_a)
    r_h, k_h, v_h, w_h, a_h = heads(r.astype(f32)), heads(k_mod), heads(v.astype(f32)), heads(decay), heads(a)

    def step(S, inp):
        r_t, w_t, k_t, v_t, kk_t, a_t = inp
        sa = jnp.einsum('bhvk,bhk->bhv', S, -kk_t)
        S = S * w_t[:, :, None, :] + sa[..., None] * (kk_t * a_t)[:, :, None, :] + v_t[..., None] * k_t[:, :, None, :]
        return S, jnp.einsum('bhvk,bhk->bhv', S, r_t)

    tm = lambda t: jnp.swapaxes(t, 0, 1)
    S_T, y_t = lax.scan(step, S0.astype(f32), (tm(r_h), tm(w_h), tm(k_h), tm(v_h), tm(kk), tm(a_h)))
    y = tm(y_t)
    mean = jnp.mean(y, axis=-1, keepdims=True)
    var = jnp.mean(jnp.square(y - mean), axis=-1, keepdims=True)
    y = (y - mean) * lax.rsqrt(var + GN_EPS) * ln_w.reshape(RWKV_HEADS, RWKV_N) + ln_b.reshape(RWKV_HEADS, RWKV_N)
    y = y + jnp.sum(r_h * k_h * r_k, axis=-1, keepdims=True) * v_h
    y = y.reshape(B, T, RWKV_W) * g
    return y.astype(p.dtype), S_T, p[:, -1]


def even_mixer(h, attend, S0, prev, w_in, w_out, q_gain, k_gain, f_bias, rwkv_params):
    B, T, _ = h.shape
    proj = jnp.einsum('btd,de->bte', h, w_in)
    fp, rp = proj[..., :FOX_IN], proj[..., FOX_IN:]
    hd = lambda t: t.reshape(B, T, FOX_HEADS, HEAD_DIM)
    q = rms_norm(hd(fp[..., :FOX_W]), q_gain)
    k = rms_norm(hd(fp[..., FOX_W:2 * FOX_W]), k_gain)
    v = hd(fp[..., 2 * FOX_W:3 * FOX_W])
    logf = jax.nn.log_sigmoid((fp[..., 3 * FOX_W:] + f_bias).astype(jnp.float32))
    o_fox = attend(q, k, v, logf)
    o_rwkv, S_T, last = rwkv7_time_mix(rp, prev, S0, *rwkv_params)
    y = jnp.einsum('bte,ed->btd', jnp.concatenate([o_fox.astype(h.dtype), o_rwkv], axis=-1), w_out)
    return y, (k, v, logf, S_T, last)


def pool_mixer(h, buf, start_pos, w_pool, pool_scale):
    B, T, D = h.shape
    f32 = jnp.float32
    ext = jnp.concatenate([buf.astype(h.dtype), h], axis=1)
    pos = start_pos + jnp.arange(POOL_BUF + T)
    ext_v = jnp.where((pos >= 0)[None, :, None], ext.astype(f32), 0.0)
    cs = jnp.concatenate([jnp.zeros((B, 1, D), f32), jnp.cumsum(ext_v, axis=1)], axis=1)
    tpos = pos[POOL_BUF:]
    groups = []
    for gi, win in enumerate(POOL_WINDOWS):
        csg = cs[..., gi * POOL_C:(gi + 1) * POOL_C]
        wsum = csg[:, POOL_BUF + 1:POOL_BUF + 1 + T] - csg[:, POOL_BUF + 1 - win:POOL_BUF + 1 - win + T]
        cnt = jnp.minimum(tpos + 1, win).astype(f32)
        groups.append(wsum / cnt[None, :, None])
    pooled = jnp.concatenate(groups, axis=-1)
    z = (pooled - h.astype(f32)).astype(h.dtype).reshape(B, T, POOL_GROUPS, POOL_C)
    y = jnp.einsum('btgc,gce->btge', z, w_pool).reshape(B, T, D) * pool_scale
    return y, ext[:, -POOL_BUF:]


def swiglu(h, wg, wu, wd):
    return jnp.einsum('btf,fd->btd', jax.nn.silu(h @ wg) * (h @ wu), wd)


def moe_ffn(h, router_w, router_b, wg, wu, wd):
    f32 = jnp.float32
    logits = jnp.einsum('btd,de->bte', h, router_w).astype(f32) + router_b
    top_v, top_i = lax.top_k(logits, TOP_K)
    gates = jax.nn.softmax(top_v, axis=-1)
    combine = jnp.einsum('btk,btke->bte', gates, jax.nn.one_hot(top_i, N_EXPERTS, dtype=f32))
    y = jnp.zeros(h.shape, f32)
    for e in range(N_EXPERTS):
        y = y + combine[..., e:e + 1] * swiglu(h, wg[e], wu[e], wd[e]).astype(f32)
    return y.astype(h.dtype)


def setup_inputs(seed: int = 0) -> dict:
    key = jax.random.key(seed)
    keys = jax.random.split(key, 64)
    counter = [0]

    def nk():
        counter[0] += 1
        return keys[counter[0] - 1]

    f32 = jnp.float32
    nrm = lambda shape, s: jax.random.normal(nk(), shape, f32) * s
    n_pages = PAST_LEN // PAGE_SIZE
    n_used = DEC_BATCH * n_pages
    n_phys = n_used + n_used // 4
    D = D_MODEL
    inp = {}
    inp['x_prompt'] = nrm((BATCH, SEQ, D), 1.0)
    inp['x_sample'] = nrm((DEC_BATCH, DEC_SEQ, D), 1.0)
    inp['c_prompt'] = nrm((BATCH, D), 1.0)
    inp['c_sample'] = nrm((DEC_BATCH, D), 1.0)
    inp['cache_fox_k'] = nrm((N_EVEN, n_phys, PAGE_SIZE, FOX_HEADS, HEAD_DIM), 1.0)
    inp['cache_fox_v'] = nrm((N_EVEN, n_phys, PAGE_SIZE, FOX_HEADS, HEAD_DIM), 1.0)
    inp['cache_fox_logf'] = jax.nn.log_sigmoid(3.0 + nrm((N_EVEN, n_phys, PAGE_SIZE, FOX_HEADS), 1.0))
    inp['page_table'] = jax.random.permutation(nk(), n_phys)[:n_used].reshape(DEC_BATCH, n_pages).astype(jnp.int32)
    inp['state_rwkv'] = nrm((N_EVEN, DEC_BATCH, RWKV_HEADS, RWKV_N, RWKV_N), 1.0)
    inp['state_rwkv_shift'] = nrm((N_EVEN, DEC_BATCH, RWKV_IN), 1.0)
    inp['state_pool'] = nrm((N_ODD, DEC_BATCH, POOL_BUF, D), 1.0)
    inp['norm1_g'] = 1.0 + nrm((DEPTH, D), 0.1)
    inp['norm2_g'] = 1.0 + nrm((DEPTH, D), 0.1)
    inp['ada_w'] = nrm((DEPTH, D, 6 * D), 0.5 * D ** -0.5)
    inp['ada_b'] = nrm((DEPTH, 6 * D), 0.02)
    inp['w_in'] = nrm((N_EVEN, D, IN_W), D ** -0.5)
    inp['w_out'] = nrm((N_EVEN, MIX_W, D), MIX_W ** -0.5)
    inp['fox_q_gain'] = 1.0 + nrm((N_EVEN, HEAD_DIM), 0.1)
    inp['fox_k_gain'] = 1.0 + nrm((N_EVEN, HEAD_DIM), 0.1)
    inp['fox_f_bias'] = 3.0 + nrm((N_EVEN, FOX_HEADS), 0.5)
    inp['rwkv_mu'] = jax.random.uniform(nk(), (N_EVEN, RWKV_IN), f32)
    inp['rwkv_w0'] = jnp.linspace(-6.5, -1.5, RWKV_W, dtype=f32)[None, :] + nrm((N_EVEN, RWKV_W), 0.1)
    inp['rwkv_w2'] = nrm((N_EVEN, W_LORA, RWKV_W), 0.1)
    inp['rwkv_a0'] = nrm((N_EVEN, RWKV_W), 0.1)
    inp['rwkv_a2'] = nrm((N_EVEN, A_LORA, RWKV_W), A_LORA ** -0.5)
    inp['rwkv_g2'] = nrm((N_EVEN, G_LORA, RWKV_W), G_LORA ** -0.5)
    inp['rwkv_k_k'] = 0.85 + nrm((N_EVEN, RWKV_W), 0.05)
    inp['rwkv_k_a'] = 1.0 + nrm((N_EVEN, RWKV_W), 0.05)
    inp['rwkv_r_k'] = nrm((N_EVEN, RWKV_HEADS, RWKV_N), 0.1)
    inp['rwkv_ln_w'] = 1.0 + nrm((N_EVEN, RWKV_W), 0.1)
    inp['rwkv_ln_b'] = nrm((N_EVEN, RWKV_W), 0.02)
    inp['ffn_w_gate'] = nrm((N_EVEN, D, D_FF), D ** -0.5)
    inp['ffn_w_up'] = nrm((N_EVEN, D, D_FF), D ** -0.5)
    inp['ffn_w_down'] = nrm((N_EVEN, D_FF, D), D_FF ** -0.5)
    inp['pool_w'] = nrm((N_ODD, POOL_GROUPS, POOL_C, POOL_C), POOL_C ** -0.5)
    inp['pool_scale'] = 1.0 + nrm((N_ODD, D), 0.1)
    inp['moe_router_w'] = nrm((N_ODD, D, N_EXPERTS), D ** -0.5)
    inp['moe_router_b'] = nrm((N_ODD, N_EXPERTS), 0.01)
    inp['moe_w_gate'] = nrm((N_ODD, N_EXPERTS, D, D_FF_EXPERT), D ** -0.5)
    inp['moe_w_up'] = nrm((N_ODD, N_EXPERTS, D, D_FF_EXPERT), D ** -0.5)
    inp['moe_w_down'] = nrm((N_ODD, N_EXPERTS, D_FF_EXPERT, D), D_FF_EXPERT ** -0.5)
    return inp


def reference(x_prompt, x_sample, c_prompt, c_sample, cache_fox_k, cache_fox_v, cache_fox_logf,
              page_table, state_rwkv, state_rwkv_shift, state_pool,
              norm1_g, norm2_g, ada_w, ada_b, w_in, w_out, fox_q_gain, fox_k_gain, fox_f_bias,
              rwkv_mu, rwkv_w0, rwkv_w2, rwkv_a0, rwkv_a2, rwkv_g2, rwkv_k_k, rwkv_k_a, rwkv_r_k,
              rwkv_ln_w, rwkv_ln_b, ffn_w_gate, ffn_w_up, ffn_w_down,
              pool_w, pool_scale, moe_router_w, moe_router_b, moe_w_gate, moe_w_up, moe_w_down):
    xp, xs = x_prompt, x_sample
    bp, bs = xp.shape[0], xs.shape[0]
    n_past = page_table.shape[1] * PAGE_SIZE
    kp_l, ks_l, vp_l, vs_l, fp_l, fs_l = [], [], [], [], [], []
    sp_l, ss_l, shp_l, shs_l, pp_l, ps_l = [], [], [], [], [], []
    for li in range(DEPTH):
        sh1p, sc1p, g1p, sh2p, sc2p, g2p = ada_params(c_prompt, ada_w[li], ada_b[li])
        sh1s, sc1s, g1s, sh2s, sc2s, g2s = ada_params(c_sample, ada_w[li], ada_b[li])
        hp = modulate(xp, norm1_g[li], sh1p, sc1p)
        hs = modulate(xs, norm1_g[li], sh1s, sc1s)
        if li % 2 == 0:
            e = li // 2
            rw = (rwkv_mu[e], rwkv_w0[e], rwkv_w2[e], rwkv_a0[e], rwkv_a2[e], rwkv_g2[e],
                  rwkv_k_k[e], rwkv_k_a[e], rwkv_r_k[e], rwkv_ln_w[e], rwkv_ln_b[e])
            shared = (w_in[e], w_out[e], fox_q_gain[e], fox_k_gain[e], fox_f_bias[e], rw)
            s0p = jnp.zeros((bp, RWKV_HEADS, RWKV_N, RWKV_N), jnp.float32)
            prevp = jnp.zeros((bp, RWKV_IN), xp.dtype)
            yp, (kq, vq, lfq, sq, lastq) = even_mixer(hp, fox_prompt_attention, s0p, prevp, *shared)
            k_past = cache_fox_k[e][page_table].reshape(bs, n_past, FOX_HEADS, HEAD_DIM)
            v_past = cache_fox_v[e][page_table].reshape(bs, n_past, FOX_HEADS, HEAD_DIM)
            lf_past = cache_fox_logf[e][page_table].reshape(bs, n_past, FOX_HEADS)
            attend_s = functools.partial(fox_sample_attention, k_past=k_past, v_past=v_past, logf_past=lf_past)
            ysm, (kn, vn, lfn, sn, lastn) = even_mixer(hs, attend_s, state_rwkv[e], state_rwkv_shift[e], *shared)
            kp_l.append(kq); ks_l.append(kn); vp_l.append(vq); vs_l.append(vn)
            fp_l.append(lfq); fs_l.append(lfn); sp_l.append(sq); ss_l.append(sn)
            shp_l.append(lastq); shs_l.append(lastn)
            xp = xp + g1p * yp
            xs = xs + g1s * ysm
            xp = xp + g2p * swiglu(modulate(xp, norm2_g[li], sh2p, sc2p), ffn_w_gate[e], ffn_w_up[e], ffn_w_down[e])
            xs = xs + g2s * swiglu(modulate(xs, norm2_g[li], sh2s, sc2s), ffn_w_gate[e], ffn_w_up[e], ffn_w_down[e])
        else:
            o = li // 2
            yp, bufp = pool_mixer(hp, jnp.zeros((bp, POOL_BUF, D_MODEL), hp.dtype), -POOL_BUF, pool_w[o], pool_scale[o])
            ysm, bufs = pool_mixer(hs, state_pool[o], n_past - POOL_BUF, pool_w[o], pool_scale[o])
            pp_l.append(bufp); ps_l.append(bufs)
            xp = xp + g1p * yp
            xs = xs + g1s * ysm
            xp = xp + g2p * moe_ffn(modulate(xp, norm2_g[li], sh2p, sc2p), moe_router_w[o], moe_router_b[o],
                                    moe_w_gate[o], moe_w_up[o], moe_w_down[o])
            xs = xs + g2s * moe_ffn(modulate(xs, norm2_g[li], sh2s, sc2s), moe_router_w[o], moe_router_b[o],
                                    moe_w_gate[o], moe_w_up[o], moe_w_down[o])
    return (xp, xs, jnp.stack(kp_l), jnp.stack(ks_l), jnp.stack(vp_l), jnp.stack(vs_l),
            jnp.stack(fp_l), jnp.stack(fs_l), jnp.stack(sp_l), jnp.stack(ss_l),
            jnp.stack(shp_l), jnp.stack(shs_l), jnp.stack(pp_l), jnp.stack(ps_l))
```

```python
import functools
import math

import jax
import jax.numpy as jnp
from jax import lax
from jax.experimental import pallas as pl
from jax.experimental.pallas import tpu as pltpu

F32 = jnp.float32
BF16 = jnp.bfloat16

NH = 8
DH = 64
HW = NH * DH
N_PAIR = NH // 2
LANE = 128
LORA_W = 64
LORA_A = 64
LORA_G = 128
RW_IN = 3 * HW + LORA_W + LORA_A + LORA_G
F_PAD = 256
PW = RW_IN + 3 * HW + F_PAD
COL_Q = RW_IN
COL_K = RW_IN + HW
COL_V = RW_IN + 2 * HW
COL_F = RW_IN + 3 * HW
PACK_W = 8 * HW
POOL_WINDOWS = (2, 4, 8, 16)
POOL_BUF = 15
HALO = 16
N_EXPERTS = 8
RMS_EPS = 1e-6
GN_EPS = 64e-5
NEG_INF = -1e30
PAGE = 128
VMEM_LIMIT = 56 * 1024 * 1024


def _cparams(*sem):
    return pltpu.CompilerParams(dimension_semantics=sem, vmem_limit_bytes=VMEM_LIMIT)


def _bdot(a, b):
    return jnp.dot(a.astype(BF16), b.astype(BF16), preferred_element_type=F32)


def _nt(a, b):
    return lax.dot_general(a.astype(BF16), b.astype(BF16), (((1,), (1,)), ((), ())),
                           preferred_element_type=F32)


def _tn(a, b):
    return lax.dot_general(a.astype(BF16), b.astype(BF16), (((0,), (0,)), ((), ())),
                           preferred_element_type=F32)


def _split(x, terms):
    out = []
    for _ in range(terms - 1):
        h = x.astype(BF16)
        out.append(h)
        x = x - h.astype(F32)
    out.append(x.astype(BF16))
    return out


def _dot_exact_lhs(a01, x, terms=3):
    a = a01.astype(BF16)
    acc = None
    for t in _split(x, terms):
        d = jnp.dot(a, t, preferred_element_type=F32)
        acc = d if acc is None else acc + d
    return acc


def _dot_exact_rhs(x, b01, terms=2):
    b = b01.astype(BF16)
    acc = None
    for t in _split(x, terms):
        d = jnp.dot(t, b, preferred_element_type=F32)
        acc = d if acc is None else acc + d
    return acc


def _dot3(a, b):
    ah, al = _split(a, 2)
    bh, bl = _split(b, 2)
    return (jnp.dot(ah, bh, preferred_element_type=F32) + jnp.dot(ah, bl, preferred_element_type=F32)
            + jnp.dot(al, bh, preferred_element_type=F32))


def _sigmoid(x):
    return 1.0 / (1.0 + jnp.exp(-x))


def _softplus(x):
    return jnp.maximum(x, 0.0) + jnp.log(1.0 + jnp.exp(-jnp.abs(x)))


def _iota(shape, dim):
    return lax.broadcasted_iota(jnp.int32, shape, dim)


def _seg_ones(n, seg):
    return (_iota((n, n), 0) // seg == _iota((n, n), 1) // seg).astype(BF16)


class _Layout:
    def __init__(self, bp, t, bs, ts):
        self.bp, self.t, self.bs, self.ts = bp, t, bs, ts
        self.np_rows = bp * t
        self.ms = bs * ts
        self.tm = min(1024, t)
        assert t % self.tm == 0 and self.tm % bs == 0 and self.ms <= self.tm
        assert self.np_rows % self.ms == 0 and bs % 8 == 0
        self.mp = self.np_rows + self.tm
        self.ntok = self.np_rows + self.ms
        self.bp_pad = 8 * ((bp + 7) // 8)
        self.mc = self.bp_pad + bs


def _tile_rows(lay, tm):
    assert lay.tm % tm == 0 and tm % lay.bs == 0
    return lay.mp // tm, lay.np_rows // tm, lay.t // tm


def _mod_params(lay, tm, i, refs):
    _, n_pt, tpb = _tile_rows(lay, tm)
    is_prompt = i < n_pt
    b = jnp.minimum(i // tpb, lay.bp - 1)
    out = []
    for r in refs:
        p_row = r[pl.ds(b, 1), :]
        s_blk = r[lay.bp_pad:lay.bp_pad + lay.bs, :]
        s_rows = jnp.concatenate([s_blk] * (tm // lay.bs), axis=0)
        out.append(jnp.where(is_prompt, p_row, s_rows))
    return out


def _rms(x, g):
    return x * lax.rsqrt(jnp.mean(x * x, axis=-1, keepdims=True) + RMS_EPS) * g


def _mod_spec(lay, li, k, nidx):
    if nidx == 1:
        return pl.BlockSpec((None, lay.mc, D_MODEL), lambda i: (li, 0, k))
    return pl.BlockSpec((None, lay.mc, D_MODEL), lambda i, j: (li, 0, k))


D_MODEL = 1024


def _ada_kernel(c_ref, w_ref, b_ref, o_ref):
    c = c_ref[...]
    s = c * _sigmoid(c)
    o_ref[...] = _dot3(s, w_ref[...]) + b_ref[...]


def _ada(c_all, ada_w, ada_b):
    nl, d, n = ada_w.shape
    mc = c_all.shape[0]
    tn = 512
    return pl.pallas_call(
        _ada_kernel,
        grid=(nl, n // tn),
        in_specs=[pl.BlockSpec((mc, d), lambda l, j: (0, 0)),
                  pl.BlockSpec((None, d, tn), lambda l, j: (l, 0, j)),
                  pl.BlockSpec((None, 1, tn), lambda l, j: (l, 0, j))],
        out_specs=pl.BlockSpec((None, mc, tn), lambda l, j: (l, 0, j)),
        out_shape=jax.ShapeDtypeStruct((nl, mc, n), F32),
        compiler_params=_cparams("arbitrary", "arbitrary"),
        name="ada",
    )(c_all, ada_w, ada_b.reshape(nl, 1, n))


def _inproj_kernel(x_ref, g_ref, sh_ref, sc_ref, w_ref, qkg_ref, fb_ref, o_ref, *, lay, tm):
    i = pl.program_id(0)
    sh, sc = _mod_params(lay, tm, i, (sh_ref, sc_ref))
    h = (_rms(x_ref[...], g_ref[...]) * (1.0 + sc) + sh).astype(BF16)
    ch = 256
    seg = _seg_ones(ch, DH)
    for c in range(PW // ch):
        c0 = c * ch
        acc = jnp.dot(h, w_ref[:, c0:c0 + ch], preferred_element_type=F32)
        if COL_Q <= c0 < COL_V:
            ss = _dot_exact_rhs(acc * acc, seg)
            acc = acc * lax.rsqrt(ss * (1.0 / DH) + RMS_EPS) * qkg_ref[:, c0 - COL_Q:c0 - COL_Q + ch]
        elif c0 >= COL_F:
            z = acc + fb_ref[...]
            acc = -_softplus(-z)
        o_ref[:, c0:c0 + ch] = acc


def _inproj(lay, x_all, mod, li, g, w_cat, qk_gain, f_bias):
    tm = min(512, lay.tm)
    nt, _, _ = _tile_rows(lay, tm)
    d = D_MODEL
    return pl.pallas_call(
        functools.partial(_inproj_kernel, lay=lay, tm=tm),
        grid=(nt,),
        in_specs=[pl.BlockSpec((tm, d), lambda i: (i, 0)),
                  pl.BlockSpec((1, d), lambda i: (0, 0)),
                  _mod_spec(lay, li, 0, 1), _mod_spec(lay, li, 1, 1),
                  pl.BlockSpec((d, PW), lambda i: (0, 0)),
                  pl.BlockSpec((1, 2 * HW), lambda i: (0, 0)),
                  pl.BlockSpec((1, F_PAD), lambda i: (0, 0))],
        out_specs=pl.BlockSpec((tm, PW), lambda i: (i, 0)),
        out_shape=jax.ShapeDtypeStruct((lay.mp, PW), F32),
        compiler_params=_cparams("arbitrary"),
        name="inproj",
    )(x_all, g, mod, mod, w_cat, qk_gain, f_bias)


def _cumf_kernel(lf_ref, fq_ref, fc_ref, carry_ref, *, tc):
    c = pl.program_id(1)

    @pl.when(c == 0)
    def _():
        carry_ref[...] = jnp.zeros_like(carry_ref)

    tri = _iota((tc, tc), 0) >= _iota((tc, tc), 1)
    f = _dot_exact_lhs(tri, lf_ref[...]) + carry_ref[...]
    carry_ref[...] = f[tc - 1:tc, :]
    fc_ref[...] = f
    expand = _iota((LANE, HW), 0) == _iota((LANE, HW), 1) // DH
    fq_ref[...] = _dot_exact_rhs(f, expand, terms=3)


def _cumf(lay, proj):
    tc = min(256, lay.t)
    npc = lay.t // tc
    return pl.pallas_call(
        functools.partial(_cumf_kernel, tc=tc),
        grid=(lay.bp, npc),
        in_specs=[pl.BlockSpec((tc, LANE), lambda b, c: (b * npc + c, COL_F // LANE))],
        out_specs=[pl.BlockSpec((tc, HW), lambda b, c: (b * npc + c, 0)),
                   pl.BlockSpec((tc, LANE), lambda b, c: (b * npc + c, 0))],
        out_shape=[jax.ShapeDtypeStruct((lay.np_rows, HW), F32),
                   jax.ShapeDtypeStruct((lay.np_rows, LANE), F32)],
        scratch_shapes=[pltpu.VMEM((1, LANE), F32)],
        compiler_params=_cparams("arbitrary", "arbitrary"),
        name="cumf",
    )(proj)


def _fox_prompt_kernel(q_ref, k_ref, v_ref, fq_ref, fk_ref, o_ref, *, tq):
    qi = pl.program_id(2)
    lane_hi = _iota((1, LANE), 1) >= DH
    q = q_ref[...] * (DH ** -0.5)
    fq = fq_ref[...]
    row = qi * tq + _iota((tq, tq), 0)
    outs = []
    for hh in range(2):
        sel = lane_hi if hh else jnp.logical_not(lane_hi)
        qh = jnp.where(sel, q, 0.0).astype(BF16)
        fqh = fq[:, hh * DH:hh * DH + 1]

        def body(kj, carry, qh=qh, fqh=fqh, hh=hh):
            m, l, acc = carry
            k0 = pl.multiple_of(kj * tq, tq)
            kb = k_ref[pl.ds(k0, tq), :].astype(BF16)
            vb = v_ref[pl.ds(k0, tq), :].astype(BF16)
            s = lax.dot_general(qh, kb, (((1,), (1,)), ((), ())), preferred_element_type=F32)
            s = s + fqh - fk_ref[hh:hh + 1, pl.ds(k0, tq)]
            col = k0 + _iota((tq, tq), 1)
            s = jnp.where(col <= row, s, NEG_INF)
            m_new = jnp.maximum(m, jnp.max(s, axis=-1, keepdims=True))
            p = jnp.exp(s - m_new)
            alpha = jnp.exp(m - m_new)
            l = alpha * l + jnp.sum(p, axis=-1, keepdims=True)
            acc = alpha * acc + jnp.dot(p.astype(BF16), vb, preferred_element_type=F32)
            return m_new, l, acc

        init = (jnp.full((tq, 1), NEG_INF, F32), jnp.zeros((tq, 1), F32), jnp.zeros((tq, LANE), F32))
        m, l, acc = lax.fori_loop(0, qi + 1, body, init)
        outs.append(acc / l)
    o_ref[...] = jnp.where(lane_hi, outs[1], outs[0])


def _fox_prompt(lay, proj, fq, fk_rows):
    tq = min(256, lay.t)
    nq = lay.t // tq
    blk = lambda col: pl.BlockSpec((tq, LANE), lambda b, p, i: (b * nq + i, col // LANE + p))
    seq = lambda col: pl.BlockSpec((lay.t, LANE), lambda b, p, i: (b, col // LANE + p))
    return pl.pallas_call(
        functools.partial(_fox_prompt_kernel, tq=tq),
        grid=(lay.bp, N_PAIR, nq),
        in_specs=[blk(COL_Q), seq(COL_K), seq(COL_V),
                  pl.BlockSpec((tq, LANE), lambda b, p, i: (b * nq + i, p)),
                  pl.BlockSpec((None, None, 2, lay.t), lambda b, p, i: (b, p, 0, 0))],
        out_specs=pl.BlockSpec((tq, LANE), lambda b, p, i: (b * nq + i, p)),
        out_shape=jax.ShapeDtypeStruct((lay.np_rows, HW), F32),
        compiler_params=_cparams("arbitrary", "arbitrary", "arbitrary"),
        name="fox_prompt",
    )(proj, proj, proj, fq, fk_rows)


def _fox_sample_kernel(pt_ref, qt_ref, kn_ref, vn_ref, lfn_ref, *rest, n_pages, ts):
    k_refs = rest[:n_pages]
    v_refs = rest[n_pages:2 * n_pages]
    lf_refs = rest[2 * n_pages:3 * n_pages]
    o_ref = rest[3 * n_pages]
    s_ref = rest[3 * n_pages + 1]
    nl = ts * NH
    tpad = kn_ref.shape[0]
    qt = qt_ref[...].astype(BF16)
    expand = _iota((NH, nl), 0) == _iota((NH, nl), 1) % NH
    lane_q = _iota((1, nl), 1) // NH

    lfe = _dot_exact_rhs(lfn_ref[...], expand, terms=3)
    t_new = _iota((tpad, nl), 0)
    fn = jnp.sum(jnp.where(t_new <= lane_q, lfe, 0.0), axis=0, keepdims=True)
    cum_new = _dot_exact_lhs(_iota((tpad, tpad), 0) >= _iota((tpad, tpad), 1), lfe)
    s_new = jnp.dot(kn_ref[...].astype(BF16), qt, preferred_element_type=F32) + fn - cum_new
    s_new = jnp.where((t_new <= lane_q) & (t_new < ts), s_new, NEG_INF)

    upper = _iota((PAGE, PAGE), 0) < _iota((PAGE, PAGE), 1)
    carry = jnp.zeros((1, nl), F32)
    m = jnp.max(s_new, axis=0, keepdims=True)
    for j in range(n_pages - 1, -1, -1):
        lfp = _dot_exact_rhs(lf_refs[j][...], expand, terms=3)
        suffix = _dot_exact_lhs(upper, lfp) + carry
        carry = carry + jnp.sum(lfp, axis=0, keepdims=True)
        s = jnp.dot(k_refs[j][...].astype(BF16), qt, preferred_element_type=F32) + fn + suffix
        s_ref[j * PAGE:(j + 1) * PAGE, :] = s
        m = jnp.maximum(m, jnp.max(s, axis=0, keepdims=True))

    p_new = jnp.exp(s_new - m)
    l = jnp.sum(p_new, axis=0, keepdims=True)
    for j in range(n_pages):
        p = jnp.exp(s_ref[j * PAGE:(j + 1) * PAGE, :] - m)
        s_ref[j * PAGE:(j + 1) * PAGE, :] = p
        l = l + jnp.sum(p, axis=0, keepdims=True)
    inv = 1.0 / l
    acc = _tn(p_new * inv, vn_ref[...])
    for j in range(n_pages):
        acc = acc + _tn(s_ref[j * PAGE:(j + 1) * PAGE, :] * inv, v_refs[j][...])
    own = _iota((NH, HW), 0) == _iota((NH, HW), 1) // DH
    rows = [jnp.sum(jnp.where(own, acc[q * NH:(q + 1) * NH, :], 0.0), axis=0, keepdims=True)
            for q in range(ts)]
    o_ref[...] = jnp.concatenate(rows, axis=0)


def _fox_sample(lay, e, page_table, qt, k_new, v_new, lf_new, cache_k, cache_v, cache_lf):
    bs, ts = lay.bs, lay.ts
    n_pages = page_table.shape[1]
    nl = ts * NH
    tpad = k_new.shape[1]
    page = lambda j: pl.BlockSpec((None, None, PAGE, HW), lambda b, pt, j=j: (e, pt[b * n_pages + j], 0, 0))
    lfpage = lambda j: pl.BlockSpec((None, None, PAGE, NH), lambda b, pt, j=j: (e, pt[b * n_pages + j], 0, 0))
    in_specs = ([pl.BlockSpec((None, HW, nl), lambda b, pt: (b, 0, 0)),
                 pl.BlockSpec((None, tpad, HW), lambda b, pt: (b, 0, 0)),
                 pl.BlockSpec((None, tpad, HW), lambda b, pt: (b, 0, 0)),
                 pl.BlockSpec((None, tpad, NH), lambda b, pt: (b, 0, 0))]
                + [page(j) for j in range(n_pages)] + [page(j) for j in range(n_pages)]
                + [lfpage(j) for j in range(n_pages)])
    grid_spec = pltpu.PrefetchScalarGridSpec(
        num_scalar_prefetch=1, grid=(bs,), in_specs=in_specs,
        out_specs=pl.BlockSpec((None, ts, HW), lambda b, pt: (b, 0, 0)),
        scratch_shapes=[pltpu.VMEM((n_pages * PAGE, nl), F32)])
    return pl.pallas_call(
        functools.partial(_fox_sample_kernel, n_pages=n_pages, ts=ts),
        grid_spec=grid_spec,
        out_shape=jax.ShapeDtypeStruct((bs, ts, HW), F32),
        compiler_params=_cparams("arbitrary"),
        name="fox_sample",
    )(page_table.reshape(-1), qt, k_new, v_new, lf_new,
      *([cache_k] * n_pages), *([cache_v] * n_pages), *([cache_lf] * n_pages))


def _rwkv_pre_kernel(p_ref, st_ref, mu_ref, w0_ref, w2_ref, a0_ref, a2_ref, g2_ref, kk_ref, ka_ref, rk_ref,
                     o_ref, carry_ref, *, lay, tm):
    i = pl.program_id(0)
    _, n_pt, tpb = _tile_rows(lay, tm)
    p = p_ref[...]

    @pl.when(i == 0)
    def _():
        carry_ref[...] = jnp.zeros_like(carry_ref)

    first = (i % tpb) == 0
    prev_row = jnp.where(first, 0.0, carry_ref[...])
    rolled = pltpu.roll(p, 1, 0)
    sh_prompt = jnp.where(_iota((tm, 1), 0) == 0, prev_row, rolled)
    sh_sample = jnp.concatenate([st_ref[...], p[:tm - lay.bs, :]], axis=0)
    shifted = jnp.where(i < n_pt, sh_prompt, sh_sample)
    carry_ref[...] = p[tm - 1:tm, :]
    xm = p + (shifted - p) * mu_ref[...]
    r = xm[:, 0:HW]
    k = xm[:, HW:2 * HW]
    v = xm[:, 2 * HW:3 * HW]
    la = xm[:, 3 * HW:3 * HW + LANE]
    gl = xm[:, 3 * HW + LANE:RW_IN]
    w = -_softplus(-(w0_ref[...] + _bdot(jnp.tanh(la), w2_ref[...]))) - 0.5
    a = _sigmoid(a0_ref[...] + _bdot(la, a2_ref[...]))
    g = _bdot(_sigmoid(gl), g2_ref[...])
    seg = _seg_ones(HW, DH)
    kkr = k * kk_ref[...]
    kk = kkr * lax.rsqrt(jnp.maximum(_dot_exact_rhs(kkr * kkr, seg), 1e-24))
    kmod = k * (1.0 + (a - 1.0) * ka_ref[...])
    bonus = _dot_exact_rhs(r * kmod * rk_ref[...], seg) * v
    for n, val in enumerate((r, -jnp.exp(w), kmod, v, kk, kk * a, g, bonus)):
        o_ref[:, n * HW:(n + 1) * HW] = val


def _rwkv_pre(lay, proj, state_shift, rw):
    tm = min(512, lay.tm)
    nt, _, _ = _tile_rows(lay, tm)
    row = lambda n: pl.BlockSpec((1, n), lambda i: (0, 0))
    full = lambda a, b: pl.BlockSpec((a, b), lambda i: (0, 0))
    return pl.pallas_call(
        functools.partial(_rwkv_pre_kernel, lay=lay, tm=tm),
        grid=(nt,),
        in_specs=[pl.BlockSpec((tm, RW_IN), lambda i: (i, 0)),
                  full(lay.bs, RW_IN), row(RW_IN), row(HW), full(LANE, HW), row(HW), full(LANE, HW),
                  full(LORA_G, HW), row(HW), row(HW), row(HW)],
        out_specs=pl.BlockSpec((tm, PACK_W), lambda i: (i, 0)),
        out_shape=jax.ShapeDtypeStruct((lay.mp, PACK_W), F32),
        scratch_shapes=[pltpu.VMEM((1, RW_IN), F32)],
        compiler_params=_cparams("arbitrary"),
        name="rwkv_pre",
    )(proj, state_shift, rw["mu"], rw["w0"], rw["w2"], rw["a0"], rw["a2"], rw["g2"], rw["k_k"], rw["k_a"],
      rw["r_k"])


def _rwkv_scan_kernel(x_ref, s0_ref, lnw_ref, lnb_ref, o_ref, so_ref, s_ref, *, c):
    ci = pl.program_id(1)

    @pl.when(ci == 0)
    def _():
        s_ref[...] = s0_ref[...]

    lane_hi = _iota((1, LANE), 1) >= DH
    tri_incl = _iota((c, c), 0) >= _iota((c, c), 1)
    rr = _iota((2 * c, 2 * c), 0)
    cc = _iota((2 * c, 2 * c), 1)
    strict = (rr % c) > (cc % c)
    incl = (rr % c) >= (cc % c)
    eye = (rr == cc).astype(F32)
    seg = _seg_ones(LANE, DH)

    def stack(x):
        return jnp.concatenate([jnp.where(lane_hi, 0.0, x), jnp.where(lane_hi, x, 0.0)], axis=0)

    for p in range(N_PAIR):
        col = lambda n: x_ref[:, n * HW + p * LANE:n * HW + (p + 1) * LANE]
        r, lw, km, v, kk, b, g, bonus = (col(n) for n in range(8))
        sbd = s_ref[p]
        cum = _dot_exact_lhs(tri_incl, lw)
        pin = jnp.exp(cum)
        a_s = stack(-kk * jnp.exp(cum - lw))
        pinv = jnp.exp(-cum)
        b_s = stack(b * pinv)
        k_s = stack(km * pinv)
        r_s = stack(r * pin)
        v_s = stack(v)
        n_ab = jnp.where(strict, _nt(a_s, b_s), 0.0)
        n_ak = jnp.where(strict, _nt(a_s, k_s), 0.0)
        n_rb = jnp.where(incl, _nt(r_s, b_s), 0.0)
        n_rk = jnp.where(incl, _nt(r_s, k_s), 0.0)
        x = eye + n_ab
        pw = n_ab
        for _ in range(max(0, math.ceil(math.log2(c)) - 1)):
            pw = _bdot(pw, pw)
            x = x + _bdot(x, pw)
        z = _nt(a_s, sbd) + _bdot(n_ak, v_s)
        u = _bdot(x, z)
        ys = _nt(r_s, sbd) + _bdot(n_rb, u) + _bdot(n_rk, v_s)
        y = ys[:c] + ys[c:]
        s_ref[p] = (sbd + _tn(u, b_s) + _tn(v_s, k_s)) * pin[c - 1:c, :]
        mean = _dot_exact_rhs(y, seg) * (1.0 / DH)
        dlt = y - mean
        var = _dot_exact_rhs(dlt * dlt, seg) * (1.0 / DH)
        yn = dlt * lax.rsqrt(var + GN_EPS) * lnw_ref[:, p * LANE:(p + 1) * LANE] + lnb_ref[:, p * LANE:(p + 1) * LANE]
        o_ref[:, p * LANE:(p + 1) * LANE] = (yn + bonus) * g

    @pl.when(ci == pl.num_programs(1) - 1)
    def _():
        so_ref[...] = s_ref[...]


def _rwkv_scan(packed, s0_bd, ln_w, ln_b, nb, tn, c):
    nc = tn // c
    return pl.pallas_call(
        functools.partial(_rwkv_scan_kernel, c=c),
        grid=(nb, nc),
        in_specs=[pl.BlockSpec((c, PACK_W), lambda b, ci: (b * nc + ci, 0)),
                  pl.BlockSpec((None, N_PAIR, LANE, LANE), lambda b, ci: (b, 0, 0, 0)),
                  pl.BlockSpec((1, HW), lambda b, ci: (0, 0)),
                  pl.BlockSpec((1, HW), lambda b, ci: (0, 0))],
        out_specs=[pl.BlockSpec((c, HW), lambda b, ci: (b * nc + ci, 0)),
                   pl.BlockSpec((None, N_PAIR, LANE, LANE), lambda b, ci: (b, 0, 0, 0))],
        out_shape=[jax.ShapeDtypeStruct((nb * tn, HW), F32),
                   jax.ShapeDtypeStruct((nb, N_PAIR, LANE, LANE), F32)],
        scratch_shapes=[pltpu.VMEM((N_PAIR, LANE, LANE), F32)],
        compiler_params=_cparams("arbitrary", "arbitrary"),
        name="rwkv_scan",
    )(packed, s0_bd, ln_w, ln_b)


def _pair_states(s):
    b = s.shape[0]
    s = s.reshape(b, N_PAIR, 2, DH, DH)
    z = jnp.zeros_like(s[:, :, 0])
    top = jnp.concatenate([s[:, :, 0], z], axis=-1)
    bot = jnp.concatenate([z, s[:, :, 1]], axis=-1)
    return jnp.concatenate([top, bot], axis=-2)


def _unpair_states(sbd):
    b = sbd.shape[0]
    return jnp.stack([sbd[:, :, :DH, :DH], sbd[:, :, DH:, DH:]], axis=2).reshape(b, NH, DH, DH)


def _outproj_kernel(x_ref, of_ref, or_ref, gt_ref, w_ref, o_ref, *, lay, tm):
    i = pl.program_id(0)
    (gate,) = _mod_params(lay, tm, i, (gt_ref,))
    y = (jnp.dot(of_ref[...].astype(BF16), w_ref[0:HW, :], preferred_element_type=F32)
         + jnp.dot(or_ref[...].astype(BF16), w_ref[HW:2 * HW, :], preferred_element_type=F32))
    o_ref[...] = x_ref[...] + gate * y


def _outproj(lay, x_all, o_fox, o_rwkv, mod, li, w_out):
    tm = lay.tm
    nt, _, _ = _tile_rows(lay, tm)
    d = D_MODEL
    return pl.pallas_call(
        functools.partial(_outproj_kernel, lay=lay, tm=tm),
        grid=(nt,),
        in_specs=[pl.BlockSpec((tm, d), lambda i: (i, 0)),
                  pl.BlockSpec((tm, HW), lambda i: (i, 0)),
                  pl.BlockSpec((tm, HW), lambda i: (i, 0)),
                  _mod_spec(lay, li, 2, 1),
                  pl.BlockSpec((2 * HW, d), lambda i: (0, 0))],
        out_specs=pl.BlockSpec((tm, d), lambda i: (i, 0)),
        out_shape=jax.ShapeDtypeStruct((lay.mp, d), F32),
        compiler_params=_cparams("arbitrary"),
        name="outproj",
    )(x_all, o_fox, o_rwkv, mod, w_out)


def _ffn_kernel(x_ref, g_ref, sh_ref, sc_ref, gt_ref, wg_ref, wu_ref, wd_ref, o_ref, h_ref, acc_ref, *, lay, tm):
    i = pl.program_id(0)
    f = pl.program_id(1)

    @pl.when(f == 0)
    def _():
        sh, sc = _mod_params(lay, tm, i, (sh_ref, sc_ref))
        h_ref[...] = (_rms(x_ref[...], g_ref[...]) * (1.0 + sc) + sh).astype(BF16)
        acc_ref[...] = jnp.zeros_like(acc_ref)

    h = h_ref[...]
    a = jnp.dot(h, wg_ref[...].astype(BF16), preferred_element_type=F32)
    u = jnp.dot(h, wu_ref[...].astype(BF16), preferred_element_type=F32)
    hid = (a * _sigmoid(a) * u).astype(BF16)
    acc_ref[...] += jnp.dot(hid, wd_ref[...].astype(BF16), preferred_element_type=F32)

    @pl.when(f == pl.num_programs(1) - 1)
    def _():
        (gate,) = _mod_params(lay, tm, i, (gt_ref,))
        o_ref[...] = x_ref[...] + gate * acc_ref[...]


def _ffn(lay, x_all, mod, li, g, wg, wu, wd):
    tm = lay.tm
    nt, _, _ = _tile_rows(lay, tm)
    d = D_MODEL
    dff = wg.shape[1]
    tf = 256
    return pl.pallas_call(
        functools.partial(_ffn_kernel, lay=lay, tm=tm),
        grid=(nt, dff // tf),
        in_specs=[pl.BlockSpec((tm, d), lambda i, f: (i, 0)),
                  pl.BlockSpec((1, d), lambda i, f: (0, 0)),
                  _mod_spec(lay, li, 3, 2), _mod_spec(lay, li, 4, 2), _mod_spec(lay, li, 5, 2),
                  pl.BlockSpec((d, tf), lambda i, f: (0, f)),
                  pl.BlockSpec((d, tf), lambda i, f: (0, f)),
                  pl.BlockSpec((tf, d), lambda i, f: (f, 0))],
        out_specs=pl.BlockSpec((tm, d), lambda i, f: (i, 0)),
        out_shape=jax.ShapeDtypeStruct((lay.mp, d), F32),
        scratch_shapes=[pltpu.VMEM((tm, d), BF16), pltpu.VMEM((tm, d), F32)],
        compiler_params=_cparams("arbitrary", "arbitrary"),
        name="ffn",
    )(x_all, g, mod, mod, mod, wg, wu, wd)


def _pool_mix(pooled_minus_h, w_ref, ps_ref):
    pc = D_MODEL // len(POOL_WINDOWS)
    ys = [jnp.dot(pooled_minus_h[gi].astype(BF16), w_ref[gi], preferred_element_type=F32)
          for gi in range(len(POOL_WINDOWS))]
    return jnp.concatenate(ys, axis=-1) * ps_ref[...]


def _pool_prompt_kernel(x_ref, g_ref, sh_ref, sc_ref, gt_ref, w_ref, ps_ref, o_ref, hl_ref, carry_ref, *, lay, tm):
    i = pl.program_id(0)
    _, _, tpb = _tile_rows(lay, tm)
    sh, sc, gate = _mod_params(lay, tm, i, (sh_ref, sc_ref, gt_ref))
    h = _rms(x_ref[...], g_ref[...]) * (1.0 + sc) + sh

    @pl.when(i == 0)
    def _():
        carry_ref[...] = jnp.zeros_like(carry_ref)

    first = (i % tpb) == 0
    halo = jnp.where(first, 0.0, carry_ref[...])
    carry_ref[...] = h[tm - HALO:, :]
    hl_ref[...] = h[tm - HALO:, :]
    ext = jnp.concatenate([halo, h], axis=0)
    pos = (i % tpb) * tm + _iota((tm, 1), 0)
    pc = D_MODEL // len(POOL_WINDOWS)
    zs = []
    for gi, win in enumerate(POOL_WINDOWS):
        s = ext[:, gi * pc:(gi + 1) * pc]
        step = 1
        while step < win:
            s = s + pltpu.roll(s, step, 0)
            step *= 2
        cnt = jnp.minimum(pos + 1, win).astype(F32)
        zs.append(s[HALO:, :] / cnt - h[:, gi * pc:(gi + 1) * pc])
    o_ref[...] = x_ref[...] + gate * _pool_mix(zs, w_ref, ps_ref)


def _pool_prompt(lay, x_all, mod, li, g, w_pool, pool_scale):
    tm = lay.tm
    _, n_pt, tpb = _tile_rows(lay, tm)
    d = D_MODEL
    pc = d // len(POOL_WINDOWS)
    return pl.pallas_call(
        functools.partial(_pool_prompt_kernel, lay=lay, tm=tm),
        grid=(n_pt,),
        in_specs=[pl.BlockSpec((tm, d), lambda i: (i, 0)),
                  pl.BlockSpec((1, d), lambda i: (0, 0)),
                  _mod_spec(lay, li, 0, 1), _mod_spec(lay, li, 1, 1), _mod_spec(lay, li, 2, 1),
                  pl.BlockSpec((len(POOL_WINDOWS), pc, pc), lambda i: (0, 0, 0)),
                  pl.BlockSpec((1, d), lambda i: (0, 0))],
        out_specs=[pl.BlockSpec((tm, d), lambda i: (i, 0)),
                   pl.BlockSpec((None, HALO, d), lambda i: (i // tpb, 0, 0))],
        out_shape=[jax.ShapeDtypeStruct((lay.mp, d), F32),
                   jax.ShapeDtypeStruct((lay.bp, HALO, d), F32)],
        scratch_shapes=[pltpu.VMEM((HALO, d), F32)],
        input_output_aliases={0: 0},
        compiler_params=_cparams("arbitrary"),
        name="pool_prompt",
    )(x_all, g, mod, mod, mod, w_pool, pool_scale)


def _pool_sample_kernel(x_ref, buf_ref, g_ref, sh_ref, sc_ref, gt_ref, w_ref, ps_ref, o_ref, h_ref, *, lay):
    bs, ts = lay.bs, lay.ts
    sl = slice(lay.bp_pad, lay.bp_pad + bs)
    sh, sc, gate = sh_ref[sl, :], sc_ref[sl, :], gt_ref[sl, :]
    pc = D_MODEL // len(POOL_WINDOWS)
    hs = []
    for t in range(ts):
        hs.append(_rms(x_ref[t * bs:(t + 1) * bs, :], g_ref[...]) * (1.0 + sc) + sh)
        h_ref[t * bs:(t + 1) * bs, :] = hs[t]
    ext = [buf_ref[e] for e in range(POOL_BUF)] + hs
    for t in range(ts):
        zs = []
        for gi, win in enumerate(POOL_WINDOWS):
            cs = slice(gi * pc, (gi + 1) * pc)
            s = ext[POOL_BUF + t][:, cs]
            for j in range(1, win):
                s = s + ext[POOL_BUF + t - j][:, cs]
            zs.append(s * (1.0 / win) - hs[t][:, cs])
        o_ref[t * bs:(t + 1) * bs, :] = x_ref[t * bs:(t + 1) * bs, :] + gate * _pool_mix(zs, w_ref, ps_ref)


def _pool_sample(lay, x_all, buf_t, mod, li, g, w_pool, pool_scale):
    d = D_MODEL
    pc = d // len(POOL_WINDOWS)
    sblk = lay.np_rows // lay.ms
    mspec = lambda k: pl.BlockSpec((None, lay.mc, d), lambda i: (li, 0, k))
    return pl.pallas_call(
        functools.partial(_pool_sample_kernel, lay=lay),
        grid=(1,),
        in_specs=[pl.BlockSpec((lay.ms, d), lambda i: (sblk, 0)),
                  pl.BlockSpec((POOL_BUF, lay.bs, d), lambda i: (0, 0, 0)),
                  pl.BlockSpec((1, d), lambda i: (0, 0)),
                  mspec(0), mspec(1), mspec(2),
                  pl.BlockSpec((len(POOL_WINDOWS), pc, pc), lambda i: (0, 0, 0)),
                  pl.BlockSpec((1, d), lambda i: (0, 0))],
        out_specs=[pl.BlockSpec((lay.ms, d), lambda i: (sblk, 0)),
                   pl.BlockSpec((lay.ms, d), lambda i: (0, 0))],
        out_shape=[jax.ShapeDtypeStruct((lay.mp, d), F32),
                   jax.ShapeDtypeStruct((lay.ms, d), F32)],
        input_output_aliases={0: 0},
        compiler_params=_cparams("arbitrary"),
        name="pool_sample",
    )(x_all, buf_t, g, mod, mod, mod, w_pool, pool_scale)


def _router_kernel(x_ref, g_ref, sh_ref, sc_ref, rw_ref, rb_ref, h_ref, meta_ref, *, lay, tm):
    i = pl.program_id(0)
    sh, sc = _mod_params(lay, tm, i, (sh_ref, sc_ref))
    h = _rms(x_ref[...], g_ref[...]) * (1.0 + sc) + sh
    h_ref[...] = h
    lane = _iota((tm, LANE), 1)
    logits = jnp.where(lane < N_EXPERTS, _dot3(h, rw_ref[...]) + rb_ref[...], NEG_INF)
    m1 = jnp.max(logits, axis=-1, keepdims=True)
    i1 = jnp.min(jnp.where(logits == m1, lane, LANE), axis=-1, keepdims=True)
    rest = jnp.where(lane == i1, NEG_INF, logits)
    m2 = jnp.max(rest, axis=-1, keepdims=True)
    i2 = jnp.min(jnp.where(rest == m2, lane, LANE), axis=-1, keepdims=True)
    e2 = jnp.exp(m2 - m1)
    g1 = 1.0 / (1.0 + e2)
    g2 = e2 / (1.0 + e2)
    meta = jnp.where(lane == 0, i1.astype(F32), 0.0)
    meta = jnp.where(lane == 1, i2.astype(F32), meta)
    meta = jnp.where(lane == 2, g1, meta)
    meta_ref[...] = jnp.where(lane == 3, g2, meta)


def _router(lay, x_all, mod, li, g, rw_pad, rb_pad):
    tm = min(512, lay.tm)
    nt, _, _ = _tile_rows(lay, tm)
    d = D_MODEL
    return pl.pallas_call(
        functools.partial(_router_kernel, lay=lay, tm=tm),
        grid=(nt,),
        in_specs=[pl.BlockSpec((tm, d), lambda i: (i, 0)),
                  pl.BlockSpec((1, d), lambda i: (0, 0)),
                  _mod_spec(lay, li, 3, 1), _mod_spec(lay, li, 4, 1),
                  pl.BlockSpec((d, LANE), lambda i: (0, 0)),
                  pl.BlockSpec((1, LANE), lambda i: (0, 0))],
        out_specs=[pl.BlockSpec((tm, d), lambda i: (i, 0)),
                   pl.BlockSpec((tm, LANE), lambda i: (i, 0))],
        out_shape=[jax.ShapeDtypeStruct((lay.mp, d), F32),
                   jax.ShapeDtypeStruct((lay.mp, LANE), F32)],
        compiler_params=_cparams("arbitrary"),
        name="router",
    )(x_all, g, mod, mod, rw_pad, rb_pad)


GATHER_ROWS = 256


def _gather_kernel(idx_ref, src_ref, o_ref, sem):
    base = pl.program_id(0) * GATHER_ROWS

    def row_copy(r):
        return pltpu.make_async_copy(src_ref.at[pl.ds(idx_ref[base + r], 1)], o_ref.at[pl.ds(r, 1)], sem)

    def start(r, c):
        row_copy(r).start()
        return c

    def wait(r, c):
        row_copy(r).wait()
        return c

    lax.fori_loop(0, GATHER_ROWS, start, 0)
    lax.fori_loop(0, GATHER_ROWS, wait, 0)


def _gather_rows(src, idx):
    n = idx.shape[0]
    assert n % GATHER_ROWS == 0
    d = src.shape[1]
    grid_spec = pltpu.PrefetchScalarGridSpec(
        num_scalar_prefetch=1, grid=(n // GATHER_ROWS,),
        in_specs=[pl.BlockSpec(memory_space=pl.ANY)],
        out_specs=pl.BlockSpec((GATHER_ROWS, d), lambda i, idx: (i, 0)),
        scratch_shapes=[pltpu.SemaphoreType.DMA(())])
    return pl.pallas_call(
        _gather_kernel, grid_spec=grid_spec,
        out_shape=jax.ShapeDtypeStruct((n, d), src.dtype),
        compiler_params=_cparams("arbitrary"),
        name="gather_rows",
    )(idx, src)


def _expert_kernel(te_ref, nu_ref, x_ref, wg_ref, wu_ref, wd_ref, o_ref, h_ref, acc_ref):
    i = pl.program_id(0)
    f = pl.program_id(1)
    used = i < nu_ref[0]

    @pl.when(jnp.logical_and(used, f == 0))
    def _():
        h_ref[...] = x_ref[...].astype(BF16)
        acc_ref[...] = jnp.zeros_like(acc_ref)

    @pl.when(used)
    def _():
        h = h_ref[...]
        a = jnp.dot(h, wg_ref[...].astype(BF16), preferred_element_type=F32)
        u = jnp.dot(h, wu_ref[...].astype(BF16), preferred_element_type=F32)
        hid = (a * _sigmoid(a) * u).astype(BF16)
        acc_ref[...] += jnp.dot(hid, wd_ref[...].astype(BF16), preferred_element_type=F32)

    @pl.when(f == pl.num_programs(1) - 1)
    def _():
        o_ref[...] = jnp.where(used, acc_ref[...], 0.0)


def _experts(xg, tile_expert, n_used, wg, wu, wd, tme):
    mg, d = xg.shape
    dff = wg.shape[2]
    tf = 512
    nf = dff // tf

    def widx(i, f, te, nu):
        return jnp.where(i < nu[0], f, nf - 1)

    grid_spec = pltpu.PrefetchScalarGridSpec(
        num_scalar_prefetch=2, grid=(mg // tme, nf),
        in_specs=[pl.BlockSpec((tme, d), lambda i, f, te, nu: (i, 0)),
                  pl.BlockSpec((None, d, tf), lambda i, f, te, nu: (te[i], 0, widx(i, f, te, nu))),
                  pl.BlockSpec((None, d, tf), lambda i, f, te, nu: (te[i], 0, widx(i, f, te, nu))),
                  pl.BlockSpec((None, tf, d), lambda i, f, te, nu: (te[i], widx(i, f, te, nu), 0))],
        out_specs=pl.BlockSpec((tme, d), lambda i, f, te, nu: (i, 0)),
        scratch_shapes=[pltpu.VMEM((tme, d), BF16), pltpu.VMEM((tme, d), F32)])
    return pl.pallas_call(
        _expert_kernel, grid_spec=grid_spec,
        out_shape=jax.ShapeDtypeStruct((mg, d), F32),
        compiler_params=_cparams("arbitrary", "arbitrary"),
        name="experts",
    )(tile_expert, n_used, xg, wg, wu, wd)


def _combine_kernel(x_ref, y1_ref, y2_ref, meta_ref, gt_ref, o_ref, *, lay, tm):
    i = pl.program_id(0)
    (gate,) = _mod_params(lay, tm, i, (gt_ref,))
    meta = meta_ref[...]
    y = meta[:, 2:3] * y1_ref[...] + meta[:, 3:4] * y2_ref[...]
    o_ref[...] = x_ref[...] + gate * y


def _combine(lay, x_all, y1, y2, meta, mod, li):
    tm = lay.tm
    nt, _, _ = _tile_rows(lay, tm)
    d = D_MODEL
    return pl.pallas_call(
        functools.partial(_combine_kernel, lay=lay, tm=tm),
        grid=(nt,),
        in_specs=[pl.BlockSpec((tm, d), lambda i: (i, 0)),
                  pl.BlockSpec((tm, d), lambda i: (i, 0)),
                  pl.BlockSpec((tm, d), lambda i: (i, 0)),
                  pl.BlockSpec((tm, LANE), lambda i: (i, 0)),
                  _mod_spec(lay, li, 5, 1)],
        out_specs=pl.BlockSpec((tm, d), lambda i: (i, 0)),
        out_shape=jax.ShapeDtypeStruct((lay.mp, d), F32),
        compiler_params=_cparams("arbitrary"),
        name="combine",
    )(x_all, y1, y2, meta, mod)


def _moe(lay, x_all, mod, li, g, router_w, router_b, wg, wu, wd):
    d = D_MODEL
    rw_pad = jnp.pad(router_w, ((0, 0), (0, LANE - N_EXPERTS)))
    rb_pad = jnp.pad(router_b, (0, LANE - N_EXPERTS)).reshape(1, LANE)
    h2, meta = _router(lay, x_all, mod, li, g, rw_pad, rb_pad)

    ntok = lay.ntok
    tme = lay.tm
    ex = meta[:ntok, 0:2].astype(jnp.int32).reshape(-1)
    onehot = (ex[:, None] == jnp.arange(N_EXPERTS)[None, :]).astype(jnp.int32)
    rank = jnp.sum((jnp.cumsum(onehot, axis=0) - onehot) * onehot, axis=1)
    counts = jnp.sum(onehot, axis=0)
    tiles = (counts + tme - 1) // tme
    tile_end = jnp.cumsum(tiles)
    offs = (tile_end - tiles) * tme
    dest = offs[ex] + rank
    n_tiles = (2 * ntok + N_EXPERTS * (tme - 1)) // tme + 1
    mg = n_tiles * tme
    src = jnp.zeros((mg,), jnp.int32).at[dest].set(jnp.arange(2 * ntok, dtype=jnp.int32) // 2)
    tile_expert = jnp.minimum(jnp.sum(jnp.arange(n_tiles)[:, None] >= tile_end[None, :], axis=1),
                              N_EXPERTS - 1).astype(jnp.int32)
    n_used = tile_end[-1:].astype(jnp.int32)
    last_e = tile_expert[jnp.maximum(n_used[0] - 1, 0)]
    tile_expert = jnp.where(jnp.arange(n_tiles) < n_used[0], tile_expert, last_e)

    xg = _gather_rows(h2, src)
    yg = _experts(xg, tile_expert, n_used, wg, wu, wd, tme)
    dest2 = jnp.pad(dest.reshape(ntok, 2), ((0, lay.mp - ntok), (0, 0)))
    y1 = _gather_rows(yg, dest2[:, 0])
    y2 = _gather_rows(yg, dest2[:, 1])
    return _combine(lay, x_all, y1, y2, meta, mod, li)


def _perm_rwkv_cols(a):
    r, wl, k, v, al, gl = jnp.split(a, [HW, HW + LORA_W, 2 * HW + LORA_W, 3 * HW + LORA_W, 3 * HW + LORA_W + LORA_A],
                                    axis=-1)
    return jnp.concatenate([r, k, v, wl, al, gl], axis=-1)


def _unperm_rwkv_cols(a):
    r, k, v, wl, al, gl = jnp.split(a, [HW, 2 * HW, 3 * HW, 3 * HW + LORA_W, 3 * HW + LORA_W + LORA_A], axis=-1)
    return jnp.concatenate([r, wl, k, v, al, gl], axis=-1)


def kernel(x_prompt, x_sample, c_prompt, c_sample, cache_fox_k, cache_fox_v, cache_fox_logf, page_table, state_rwkv, state_rwkv_shift, state_pool, norm1_g, norm2_g, ada_w, ada_b, w_in, w_out, fox_q_gain, fox_k_gain, fox_f_bias, rwkv_mu, rwkv_w0, rwkv_w2, rwkv_a0, rwkv_a2, rwkv_g2, rwkv_k_k, rwkv_k_a, rwkv_r_k, rwkv_ln_w, rwkv_ln_b, ffn_w_gate, ffn_w_up, ffn_w_down, pool_w, pool_scale, moe_router_w, moe_router_b, moe_w_gate, moe_w_up, moe_w_down):
    bp, t, d = x_prompt.shape
    bs, ts, _ = x_sample.shape
    assert d == D_MODEL
    lay = _Layout(bp, t, bs, ts)
    depth = ada_w.shape[0]
    n_phys = cache_fox_k.shape[1]
    fox_in = 3 * HW + NH

    xs_tb = jnp.swapaxes(x_sample, 0, 1).reshape(lay.ms, d)
    x_all = jnp.concatenate([x_prompt.reshape(lay.np_rows, d), xs_tb, jnp.zeros((lay.mp - lay.ntok, d), F32)], axis=0)
    c_all = jnp.concatenate([c_prompt, jnp.zeros((lay.bp_pad - bp, d), F32), c_sample], axis=0)
    mod = _ada(c_all, ada_w, ada_b)

    cache_k = cache_fox_k.reshape(cache_fox_k.shape[0], n_phys, PAGE, HW)
    cache_v = cache_fox_v.reshape(cache_fox_v.shape[0], n_phys, PAGE, HW)
    head_of_lane = jnp.arange(HW) // DH
    qmask = (head_of_lane[:, None] == jnp.arange(NH)[None, :]).astype(F32)
    tpad = 8

    outs = {k: [] for k in ("kp", "ks", "vp", "vs", "fp", "fs", "sp", "ss", "shp", "shs", "pp", "ps")}
    for li in range(depth):
        g1 = norm1_g[li].reshape(1, d)
        g2 = norm2_g[li].reshape(1, d)
        if li % 2 == 0:
            e = li // 2
            wi = w_in[e]
            w_cat = jnp.concatenate(
                [_perm_rwkv_cols(wi[:, fox_in:]), wi[:, :3 * HW],
                 jnp.pad(wi[:, 3 * HW:fox_in], ((0, 0), (0, F_PAD - NH)))], axis=1).astype(BF16)
            qk_gain = jnp.concatenate([jnp.tile(fox_q_gain[e], NH), jnp.tile(fox_k_gain[e], NH)]).reshape(1, 2 * HW)
            f_bias = jnp.pad(fox_f_bias[e], (0, F_PAD - NH)).reshape(1, F_PAD)
            proj = _inproj(lay, x_all, mod, li, g1, w_cat, qk_gain, f_bias)

            fq, fk = _cumf(lay, proj)
            fk_rows = jnp.swapaxes(fk[:, :NH].reshape(bp, t, NH), 1, 2).reshape(bp, N_PAIR, 2, t)
            o_fox_p = _fox_prompt(lay, proj, fq, fk_rows)

            ps_rows = proj[lay.np_rows:lay.ntok].reshape(ts, bs, PW).swapaxes(0, 1)
            q_s = ps_rows[..., COL_Q:COL_K] * (DH ** -0.5)
            k_s = ps_rows[..., COL_K:COL_V]
            v_s = ps_rows[..., COL_V:COL_F]
            lf_s = ps_rows[..., COL_F:COL_F + NH]
            qt = (jnp.swapaxes(q_s, 1, 2)[:, :, :, None] * qmask[None, :, None, :]).reshape(bs, HW, ts * NH)
            padt = lambda a: jnp.pad(a, ((0, 0), (0, tpad - ts), (0, 0)))
            o_fox_s = _fox_sample(lay, e, page_table, qt, padt(k_s), padt(v_s), padt(lf_s),
                                  cache_k, cache_v, cache_fox_logf)
            o_fox = jnp.concatenate([o_fox_p, jnp.swapaxes(o_fox_s, 0, 1).reshape(lay.ms, HW),
                                     jnp.zeros((lay.mp - lay.ntok, HW), F32)], axis=0)

            pad_lora = lambda w_, top: jnp.pad(w_, ((0, LANE - w_.shape[0]), (0, 0)) if top else
                                               ((LANE - w_.shape[0], 0), (0, 0))).astype(BF16)
            rw = dict(mu=_perm_rwkv_cols(rwkv_mu[e]).reshape(1, RW_IN), w0=rwkv_w0[e].reshape(1, HW),
                      w2=pad_lora(rwkv_w2[e], True), a0=rwkv_a0[e].reshape(1, HW), a2=pad_lora(rwkv_a2[e], False),
                      g2=rwkv_g2[e].astype(BF16), k_k=rwkv_k_k[e].reshape(1, HW), k_a=rwkv_k_a[e].reshape(1, HW),
                      r_k=rwkv_r_k[e].reshape(1, HW))
            packed = _rwkv_pre(lay, proj, _perm_rwkv_cols(state_rwkv_shift[e]), rw)
            ln_w = rwkv_ln_w[e].reshape(1, HW)
            ln_b = rwkv_ln_b[e].reshape(1, HW)
            cp = min(64, t)
            o_rw_p, s_p = _rwkv_scan(packed[:lay.np_rows], jnp.zeros((bp, N_PAIR, LANE, LANE), F32), ln_w, ln_b,
                                     bp, t, cp)
            pk_s = packed[lay.np_rows:lay.ntok].reshape(ts, bs, PACK_W).swapaxes(0, 1)
            pk_s = jnp.pad(pk_s, ((0, 0), (0, tpad - ts), (0, 0))).reshape(bs * tpad, PACK_W)
            o_rw_s, s_s = _rwkv_scan(pk_s, _pair_states(state_rwkv[e]), ln_w, ln_b, bs, tpad, tpad)
            o_rw_s = o_rw_s.reshape(bs, tpad, HW)[:, :ts].swapaxes(0, 1).reshape(lay.ms, HW)
            o_rwkv = jnp.concatenate([o_rw_p, o_rw_s, jnp.zeros((lay.mp - lay.ntok, HW), F32)], axis=0)

            x_all = _outproj(lay, x_all, o_fox, o_rwkv, mod, li, w_out[e].astype(BF16))
            x_all = _ffn(lay, x_all, mod, li, g2, ffn_w_gate[e], ffn_w_up[e], ffn_w_down[e])

            outs["kp"].append(proj[:lay.np_rows, COL_K:COL_V].reshape(bp, t, NH, DH))
            outs["vp"].append(proj[:lay.np_rows, COL_V:COL_F].reshape(bp, t, NH, DH))
            outs["fp"].append(proj[:lay.np_rows, COL_F:COL_F + NH].reshape(bp, t, NH))
            outs["ks"].append(k_s.reshape(bs, ts, NH, DH))
            outs["vs"].append(v_s.reshape(bs, ts, NH, DH))
            outs["fs"].append(lf_s)
            outs["sp"].append(_unpair_states(s_p))
            outs["ss"].append(_unpair_states(s_s))
            last_p = proj[:lay.np_rows, :RW_IN].reshape(bp, t, RW_IN)[:, -1]
            last_s = proj[lay.np_rows + (ts - 1) * bs:lay.ntok, :RW_IN]
            outs["shp"].append(_unperm_rwkv_cols(last_p))
            outs["shs"].append(_unperm_rwkv_cols(last_s))
        else:
            o = li // 2
            w_pool = pool_w[o].astype(BF16)
            p_scale = pool_scale[o].reshape(1, d)
            x_all, hl = _pool_prompt(lay, x_all, mod, li, g1, w_pool, p_scale)
            buf_t = jnp.swapaxes(state_pool[o], 0, 1)
            x_all, h_s = _pool_sample(lay, x_all, buf_t, mod, li, g1, w_pool, p_scale)
            outs["pp"].append(hl[:, HALO - POOL_BUF:, :])
            h_s_bt = jnp.swapaxes(h_s.reshape(ts, bs, d), 0, 1)
            outs["ps"].append(jnp.concatenate([state_pool[o], h_s_bt], axis=1)[:, -POOL_BUF:])
            x_all = _moe(lay, x_all, mod, li, g2, moe_router_w[o], moe_router_b[o],
                         moe_w_gate[o], moe_w_up[o], moe_w_down[o])

    y_p = x_all[:lay.np_rows].reshape(bp, t, d)
    y_s = jnp.swapaxes(x_all[lay.np_rows:lay.ntok].reshape(ts, bs, d), 0, 1)
    st = lambda k: jnp.stack(outs[k])
    return (y_p, y_s, st("kp"), st("ks"), st("vp"), st("vs"), st("fp"), st("fs"), st("sp"), st("ss"),
            st("shp"), st("shs"), st("pp"), st("ps"))
```

```python
import functools
import math

import jax
import jax.numpy as jnp
from jax import lax
from jax.experimental import pallas as pl
from jax.experimental.pallas import tpu as pltpu

F32 = jnp.float32
BF16 = jnp.bfloat16

NH = 8
DH = 64
HW = NH * DH
N_PAIR = NH // 2
LANE = 128
LORA_W = 64
LORA_A = 64
LORA_G = 128
RW_IN = 3 * HW + LORA_W + LORA_A + LORA_G
F_PAD = 256
PW = RW_IN + 3 * HW + F_PAD
COL_Q = RW_IN
COL_K = RW_IN + HW
COL_V = RW_IN + 2 * HW
COL_F = RW_IN + 3 * HW
PACK_W = 8 * HW
POOL_WINDOWS = (2, 4, 8, 16)
POOL_BUF = 15
HALO = 16
N_EXPERTS = 8
RMS_EPS = 1e-6
GN_EPS = 64e-5
NEG_INF = -1e30
PAGE = 128
VMEM_LIMIT = 56 * 1024 * 1024


def _cparams(*sem):
    return pltpu.CompilerParams(dimension_semantics=sem, vmem_limit_bytes=VMEM_LIMIT)


def _bdot(a, b):
    return jnp.dot(a.astype(BF16), b.astype(BF16), preferred_element_type=F32)


def _nt(a, b):
    return lax.dot_general(a.astype(BF16), b.astype(BF16), (((1,), (1,)), ((), ())),
                           preferred_element_type=F32)


def _tn(a, b):
    return lax.dot_general(a.astype(BF16), b.astype(BF16), (((0,), (0,)), ((), ())),
                           preferred_element_type=F32)


def _split(x, terms):
    out = []
    for _ in range(terms - 1):
        h = x.astype(BF16)
        out.append(h)
        x = x - h.astype(F32)
    out.append(x.astype(BF16))
    return out


def _dot_exact_lhs(a01, x, terms=3):
    a = a01.astype(BF16)
    acc = None
    for t in _split(x, terms):
        d = jnp.dot(a, t, preferred_element_type=F32)
        acc = d if acc is None else acc + d
    return acc


def _dot_exact_rhs(x, b01, terms=2):
    b = b01.astype(BF16)
    acc = None
    for t in _split(x, terms):
        d = jnp.dot(t, b, preferred_element_type=F32)
        acc = d if acc is None else acc + d
    return acc


def _dot3(a, b):
    ah, al = _split(a, 2)
    bh, bl = _split(b, 2)
    return (jnp.dot(ah, bh, preferred_element_type=F32) + jnp.dot(ah, bl, preferred_element_type=F32)
            + jnp.dot(al, bh, preferred_element_type=F32))


def _sigmoid(x):
    return 1.0 / (1.0 + jnp.exp(-x))


def _softplus(x):
    return jnp.maximum(x, 0.0) + jnp.log(1.0 + jnp.exp(-jnp.abs(x)))


def _iota(shape, dim):
    return lax.broadcasted_iota(jnp.int32, shape, dim)


def _seg_ones(n, seg):
    return (_iota((n, n), 0) // seg == _iota((n, n), 1) // seg).astype(BF16)


class _Layout:
    def __init__(self, bp, t, bs, ts):
        self.bp, self.t, self.bs, self.ts = bp, t, bs, ts
        self.np_rows = bp * t
        self.ms = bs * ts
        self.tm = min(1024, t)
        assert t % self.tm == 0 and self.tm % bs == 0 and self.ms <= self.tm
        assert self.np_rows % self.ms == 0 and bs % 8 == 0
        self.mp = self.np_rows + self.tm
        self.ntok = self.np_rows + self.ms
        self.bp_pad = 8 * ((bp + 7) // 8)
        self.mc = self.bp_pad + bs


def _tile_rows(lay, tm):
    assert lay.tm % tm == 0 and tm % lay.bs == 0
    return lay.mp // tm, lay.np_rows // tm, lay.t // tm


def _mod_params(lay, tm, i, refs):
    _, n_pt, tpb = _tile_rows(lay, tm)
    is_prompt = i < n_pt
    b = jnp.minimum(i // tpb, lay.bp - 1)
    out = []
    for r in refs:
        p_row = r[pl.ds(b, 1), :]
        s_blk = r[lay.bp_pad:lay.bp_pad + lay.bs, :]
        s_rows = jnp.concatenate([s_blk] * (tm // lay.bs), axis=0)
        out.append(jnp.where(is_prompt, p_row, s_rows))
    return out


def _rms(x, g):
    return x * lax.rsqrt(jnp.mean(x * x, axis=-1, keepdims=True) + RMS_EPS) * g


def _mod_spec(lay, li, k, nidx):
    if nidx == 1:
        return pl.BlockSpec((None, lay.mc, D_MODEL), lambda i: (li, 0, k))
    return pl.BlockSpec((None, lay.mc, D_MODEL), lambda i, j: (li, 0, k))


D_MODEL = 1024


def _ada_kernel(c_ref, w_ref, b_ref, o_ref):
    c = c_ref[...]
    s = c * _sigmoid(c)
    o_ref[...] = _dot3(s, w_ref[...]) + b_ref[...]


def _ada(c_all, ada_w, ada_b):
    nl, d, n = ada_w.shape
    mc = c_all.shape[0]
    tn = 512
    return pl.pallas_call(
        _ada_kernel,
        grid=(nl, n // tn),
        in_specs=[pl.BlockSpec((mc, d), lambda l, j: (0, 0)),
                  pl.BlockSpec((None, d, tn), lambda l, j: (l, 0, j)),
                  pl.BlockSpec((None, 1, tn), lambda l, j: (l, 0, j))],
        out_specs=pl.BlockSpec((None, mc, tn), lambda l, j: (l, 0, j)),
        out_shape=jax.ShapeDtypeStruct((nl, mc, n), F32),
        compiler_params=_cparams("arbitrary", "arbitrary"),
        name="ada",
    )(c_all, ada_w, ada_b.reshape(nl, 1, n))


def _inproj_kernel(x_ref, g_ref, sh_ref, sc_ref, w_ref, qkg_ref, fb_ref, o_ref, *, lay, tm):
    i = pl.program_id(0)
    sh, sc = _mod_params(lay, tm, i, (sh_ref, sc_ref))
    h = (_rms(x_ref[...], g_ref[...]) * (1.0 + sc) + sh).astype(BF16)
    ch = 256
    seg = _seg_ones(ch, DH)
    for c in range(PW // ch):
        c0 = c * ch
        acc = jnp.dot(h, w_ref[:, c0:c0 + ch], preferred_element_type=F32)
        if COL_Q <= c0 < COL_V:
            ss = _dot_exact_rhs(acc * acc, seg)
            acc = acc * lax.rsqrt(ss * (1.0 / DH) + RMS_EPS) * qkg_ref[:, c0 - COL_Q:c0 - COL_Q + ch]
        elif c0 >= COL_F:
            z = acc + fb_ref[...]
            acc = -_softplus(-z)
        o_ref[:, c0:c0 + ch] = acc


def _inproj(lay, x_all, mod, li, g, w_cat, qk_gain, f_bias):
    tm = min(512, lay.tm)
    nt, _, _ = _tile_rows(lay, tm)
    d = D_MODEL
    return pl.pallas_call(
        functools.partial(_inproj_kernel, lay=lay, tm=tm),
        grid=(nt,),
        in_specs=[pl.BlockSpec((tm, d), lambda i: (i, 0)),
                  pl.BlockSpec((1, d), lambda i: (0, 0)),
                  _mod_spec(lay, li, 0, 1), _mod_spec(lay, li, 1, 1),
                  pl.BlockSpec((d, PW), lambda i: (0, 0)),
                  pl.BlockSpec((1, 2 * HW), lambda i: (0, 0)),
                  pl.BlockSpec((1, F_PAD), lambda i: (0, 0))],
        out_specs=pl.BlockSpec((tm, PW), lambda i: (i, 0)),
        out_shape=jax.ShapeDtypeStruct((lay.mp, PW), F32),
        compiler_params=_cparams("arbitrary"),
        name="inproj",
    )(x_all, g, mod, mod, w_cat, qk_gain, f_bias)


def _cumf_kernel(lf_ref, fq_ref, fc_ref, carry_ref, *, tc):
    c = pl.program_id(1)

    @pl.when(c == 0)
    def _():
        carry_ref[...] = jnp.zeros_like(carry_ref)

    tri = _iota((tc, tc), 0) >= _iota((tc, tc), 1)
    f = _dot_exact_lhs(tri, lf_ref[...]) + carry_ref[...]
    carry_ref[...] = f[tc - 1:tc, :]
    fc_ref[...] = f
    expand = _iota((LANE, HW), 0) == _iota((LANE, HW), 1) // DH
    fq_ref[...] = _dot_exact_rhs(f, expand, terms=3)


def _cumf(lay, proj):
    tc = min(256, lay.t)
    npc = lay.t // tc
    return pl.pallas_call(
        functools.partial(_cumf_kernel, tc=tc),
        grid=(lay.bp, npc),
        in_specs=[pl.BlockSpec((tc, LANE), lambda b, c: (b * npc + c, COL_F // LANE))],
        out_specs=[pl.BlockSpec((tc, HW), lambda b, c: (b * npc + c, 0)),
                   pl.BlockSpec((tc, LANE), lambda b, c: (b * npc + c, 0))],
        out_shape=[jax.ShapeDtypeStruct((lay.np_rows, HW), F32),
                   jax.ShapeDtypeStruct((lay.np_rows, LANE), F32)],
        scratch_shapes=[pltpu.VMEM((1, LANE), F32)],
        compiler_params=_cparams("arbitrary", "arbitrary"),
        name="cumf",
    )(proj)


def _fox_prompt_kernel(q_ref, k_ref, v_ref, fq_ref, fk_ref, o_ref, *, tq):
    qi = pl.program_id(2)
    log2e = 1.4426950408889634
    lane_hi = _iota((1, LANE), 1) >= DH
    q = q_ref[...] * (DH ** -0.5 * log2e)
    fq = fq_ref[...] * log2e
    qh = [jnp.where(lane_hi, 0.0, q).astype(BF16), jnp.where(lane_hi, q, 0.0).astype(BF16)]
    fqh = [fq[:, 0:1], fq[:, DH:DH + 1]]
    below_diag = _iota((tq, tq), 1) <= _iota((tq, tq), 0)

    def step(kj, carry, diagonal):
        k0 = pl.multiple_of(kj * tq, tq)
        kb = k_ref[pl.ds(k0, tq), :].astype(BF16)
        vb = v_ref[pl.ds(k0, tq), :].astype(BF16)
        out = []
        for hh in range(2):
            m, l, acc = carry[hh]
            s = lax.dot_general(qh[hh], kb, (((1,), (1,)), ((), ())), preferred_element_type=F32)
            s = s + fqh[hh] - fk_ref[hh:hh + 1, pl.ds(k0, tq)] * log2e
            if diagonal:
                s = jnp.where(below_diag, s, NEG_INF)
            m_new = jnp.maximum(m, jnp.max(s, axis=-1, keepdims=True))
            p = jnp.exp2(s - m_new)
            alpha = jnp.exp2(m - m_new)
            l = alpha * l + jnp.sum(p, axis=-1, keepdims=True)
            acc = alpha * acc + jnp.dot(p.astype(BF16), vb, preferred_element_type=F32)
            out.append((m_new, l, acc))
        return tuple(out)

    init = (jnp.full((tq, 1), NEG_INF, F32), jnp.zeros((tq, 1), F32), jnp.zeros((tq, LANE), F32))
    carry = lax.fori_loop(0, qi, lambda kj, c: step(kj, c, False), (init, init))
    (_, l0, acc0), (_, l1, acc1) = step(qi, carry, True)
    o_ref[...] = jnp.where(lane_hi, acc1 / l1, acc0 / l0)


def _fox_prompt(lay, proj, fq, fk_rows):
    tq = min(512, lay.t)
    nq = lay.t // tq
    blk = lambda col: pl.BlockSpec((tq, LANE), lambda b, p, i: (b * nq + i, col // LANE + p))
    seq = lambda col: pl.BlockSpec((lay.t, LANE), lambda b, p, i: (b, col // LANE + p))
    return pl.pallas_call(
        functools.partial(_fox_prompt_kernel, tq=tq),
        grid=(lay.bp, N_PAIR, nq),
        in_specs=[blk(COL_Q), seq(COL_K), seq(COL_V),
                  pl.BlockSpec((tq, LANE), lambda b, p, i: (b * nq + i, p)),
                  pl.BlockSpec((None, None, 2, lay.t), lambda b, p, i: (b, p, 0, 0))],
        out_specs=pl.BlockSpec((tq, LANE), lambda b, p, i: (b * nq + i, p)),
        out_shape=jax.ShapeDtypeStruct((lay.np_rows, HW), F32),
        compiler_params=_cparams("arbitrary", "arbitrary", "arbitrary"),
        name="fox_prompt",
    )(proj, proj, proj, fq, fk_rows)


def _fox_sample_kernel(pt_ref, qx_ref, knt_ref, vn_ref, lfnt_ref, *rest, n_pages, ts):
    kt_refs = rest[:n_pages]
    vt_refs = rest[n_pages:2 * n_pages]
    lft_refs = rest[2 * n_pages:3 * n_pages]
    o_ref = rest[3 * n_pages]
    s_ref = rest[3 * n_pages + 1]
    nr = ts * NH
    tpad = vn_ref.shape[0]
    qx = qx_ref[...].astype(BF16)
    per_q = lambda a: jnp.concatenate([a] * ts, axis=0)
    row_q = _iota((nr, 1), 0) // NH

    cum_new = per_q(_dot_exact_rhs(lfnt_ref[...], _iota((tpad, tpad), 0) <= _iota((tpad, tpad), 1), terms=3))
    t_new = _iota((nr, tpad), 1)
    fn = jnp.sum(jnp.where(t_new == row_q, cum_new, 0.0), axis=1, keepdims=True)
    s_new = jnp.dot(qx, knt_ref[...].astype(BF16), preferred_element_type=F32) + fn - cum_new
    s_new = jnp.where((t_new <= row_q) & (t_new < ts), s_new, NEG_INF)

    later = _iota((PAGE, PAGE), 0) > _iota((PAGE, PAGE), 1)
    carry = jnp.zeros((nr, 1), F32)
    m = jnp.max(s_new, axis=1, keepdims=True)
    for j in range(n_pages - 1, -1, -1):
        lft = lft_refs[j][...]
        suffix = per_q(_dot_exact_rhs(lft, later, terms=3)) + carry
        carry = carry + per_q(jnp.sum(lft, axis=1, keepdims=True))
        s = jnp.dot(qx, kt_refs[j][...].astype(BF16), preferred_element_type=F32) + fn + suffix
        s_ref[:, j * PAGE:(j + 1) * PAGE] = s
        m = jnp.maximum(m, jnp.max(s, axis=1, keepdims=True))

    p_new = jnp.exp(s_new - m)
    p_all = jnp.exp(s_ref[...] - m)
    inv = 1.0 / (jnp.sum(p_new, axis=1, keepdims=True) + jnp.sum(p_all, axis=1, keepdims=True))
    acc = _bdot(p_new * inv, vn_ref[...])
    p_all = (p_all * inv).astype(BF16)
    for j in range(n_pages):
        acc = acc + _nt(p_all[:, j * PAGE:(j + 1) * PAGE], vt_refs[j][...])
    own = _iota((NH, HW), 0) == _iota((NH, HW), 1) // DH
    rows = [jnp.sum(jnp.where(own, acc[q * NH:(q + 1) * NH, :], 0.0), axis=0, keepdims=True)
            for q in range(ts)]
    o_ref[...] = jnp.concatenate(rows, axis=0)


def _fox_sample(lay, e, page_table, qx, knt, v_new, lfnt, cache_kt, cache_vt, cache_lft):
    bs, ts = lay.bs, lay.ts
    n_pages = page_table.shape[1]
    nr = ts * NH
    tpad = v_new.shape[1]
    page = lambda j: pl.BlockSpec((None, None, HW, PAGE), lambda b, pt, j=j: (e, pt[b * n_pages + j], 0, 0))
    lfpage = lambda j: pl.BlockSpec((None, None, NH, PAGE), lambda b, pt, j=j: (e, pt[b * n_pages + j], 0, 0))
    in_specs = ([pl.BlockSpec((None, nr, HW), lambda b, pt: (b, 0, 0)),
                 pl.BlockSpec((None, HW, tpad), lambda b, pt: (b, 0, 0)),
                 pl.BlockSpec((None, tpad, HW), lambda b, pt: (b, 0, 0)),
                 pl.BlockSpec((None, NH, tpad), lambda b, pt: (b, 0, 0))]
                + [page(j) for j in range(n_pages)] + [page(j) for j in range(n_pages)]
                + [lfpage(j) for j in range(n_pages)])
    grid_spec = pltpu.PrefetchScalarGridSpec(
        num_scalar_prefetch=1, grid=(bs,), in_specs=in_specs,
        out_specs=pl.BlockSpec((None, ts, HW), lambda b, pt: (b, 0, 0)),
        scratch_shapes=[pltpu.VMEM((nr, n_pages * PAGE), F32)])
    return pl.pallas_call(
        functools.partial(_fox_sample_kernel, n_pages=n_pages, ts=ts),
        grid_spec=grid_spec,
        out_shape=jax.ShapeDtypeStruct((bs, ts, HW), F32),
        compiler_params=_cparams("arbitrary"),
        name="fox_sample",
    )(page_table.reshape(-1), qx, knt, v_new, lfnt,
      *([cache_kt] * n_pages), *([cache_vt] * n_pages), *([cache_lft] * n_pages))


def _rwkv_pre_kernel(p_ref, st_ref, mu_ref, w0_ref, w2_ref, a0_ref, a2_ref, g2_ref, kk_ref, ka_ref, rk_ref,
                     o_ref, carry_ref, *, lay, tm):
    i = pl.program_id(0)
    _, n_pt, tpb = _tile_rows(lay, tm)
    p = p_ref[...]

    @pl.when(i == 0)
    def _():
        carry_ref[...] = jnp.zeros_like(carry_ref)

    first = (i % tpb) == 0
    prev_row = jnp.where(first, 0.0, carry_ref[...])
    rolled = pltpu.roll(p, 1, 0)
    sh_prompt = jnp.where(_iota((tm, 1), 0) == 0, prev_row, rolled)
    sh_sample = jnp.concatenate([st_ref[...], p[:tm - lay.bs, :]], axis=0)
    shifted = jnp.where(i < n_pt, sh_prompt, sh_sample)
    carry_ref[...] = p[tm - 1:tm, :]
    xm = p + (shifted - p) * mu_ref[...]
    r = xm[:, 0:HW]
    k = xm[:, HW:2 * HW]
    v = xm[:, 2 * HW:3 * HW]
    la = xm[:, 3 * HW:3 * HW + LANE]
    gl = xm[:, 3 * HW + LANE:RW_IN]
    w = -_softplus(-(w0_ref[...] + _bdot(jnp.tanh(la), w2_ref[...]))) - 0.5
    a = _sigmoid(a0_ref[...] + _bdot(la, a2_ref[...]))
    g = _bdot(_sigmoid(gl), g2_ref[...])
    seg = _seg_ones(HW, DH)
    kkr = k * kk_ref[...]
    kk = kkr * lax.rsqrt(jnp.maximum(_dot_exact_rhs(kkr * kkr, seg), 1e-24))
    kmod = k * (1.0 + (a - 1.0) * ka_ref[...])
    bonus = _dot_exact_rhs(r * kmod * rk_ref[...], seg) * v
    for n, val in enumerate((r, -jnp.exp(w), kmod, v, kk, kk * a, g, bonus)):
        o_ref[:, n * HW:(n + 1) * HW] = val


def _rwkv_pre(lay, proj, state_shift, rw):
    tm = min(512, lay.tm)
    nt, _, _ = _tile_rows(lay, tm)
    row = lambda n: pl.BlockSpec((1, n), lambda i: (0, 0))
    full = lambda a, b: pl.BlockSpec((a, b), lambda i: (0, 0))
    return pl.pallas_call(
        functools.partial(_rwkv_pre_kernel, lay=lay, tm=tm),
        grid=(nt,),
        in_specs=[pl.BlockSpec((tm, RW_IN), lambda i: (i, 0)),
                  full(lay.bs, RW_IN), row(RW_IN), row(HW), full(LANE, HW), row(HW), full(LANE, HW),
                  full(LORA_G, HW), row(HW), row(HW), row(HW)],
        out_specs=pl.BlockSpec((tm, PACK_W), lambda i: (i, 0)),
        out_shape=jax.ShapeDtypeStruct((lay.mp, PACK_W), F32),
        scratch_shapes=[pltpu.VMEM((1, RW_IN), F32)],
        compiler_params=_cparams("arbitrary"),
        name="rwkv_pre",
    )(proj, state_shift, rw["mu"], rw["w0"], rw["w2"], rw["a0"], rw["a2"], rw["g2"], rw["k_k"], rw["k_a"],
      rw["r_k"])


def _rwkv_scan_kernel(*refs, c, nbb):
    x_refs = refs[:nbb]
    s0_ref, lnw_ref, lnb_ref, o_ref, so_ref, s_ref = refs[nbb:]
    ci = pl.program_id(1)

    @pl.when(ci == 0)
    def _():
        s_ref[...] = s0_ref[...]

    lane_hi = _iota((1, LANE), 1) >= DH
    tri_incl = _iota((c, c), 0) >= _iota((c, c), 1)
    rr = _iota((2 * c, 2 * c), 0)
    cc = _iota((2 * c, 2 * c), 1)
    strict = (rr % c) > (cc % c)
    incl = (rr % c) >= (cc % c)
    eye = (rr == cc).astype(F32)
    seg = _seg_ones(LANE, DH)

    def stack(x):
        return jnp.concatenate([jnp.where(lane_hi, 0.0, x), jnp.where(lane_hi, x, 0.0)], axis=0)

    chains = [(k, p) for k in range(nbb) for p in range(N_PAIR)]
    each = lambda f, *lists: [f(*args) for args in zip(*lists)]
    col = lambda n: [x_refs[k][:, n * HW + p * LANE:n * HW + (p + 1) * LANE] for k, p in chains]
    r, lw, km, v, kk, b, g, bonus = (col(n) for n in range(8))
    sbd = [s_ref[k, p] for k, p in chains]
    cum = each(lambda x: _dot_exact_lhs(tri_incl, x), lw)
    pin = each(jnp.exp, cum)
    pinv = each(lambda x: jnp.exp(-x), cum)
    a_s = each(lambda kk_, cm, lw_: stack(-kk_ * jnp.exp(cm - lw_)), kk, cum, lw)
    b_s = each(lambda x, q: stack(x * q), b, pinv)
    k_s = each(lambda x, q: stack(x * q), km, pinv)
    r_s = each(lambda x, q: stack(x * q), r, pin)
    v_s = each(stack, v)
    n_ab = each(lambda x, y: jnp.where(strict, _nt(x, y), 0.0), a_s, b_s)
    n_ak = each(lambda x, y: jnp.where(strict, _nt(x, y), 0.0), a_s, k_s)
    n_rb = each(lambda x, y: jnp.where(incl, _nt(x, y), 0.0), r_s, b_s)
    n_rk = each(lambda x, y: jnp.where(incl, _nt(x, y), 0.0), r_s, k_s)
    z = each(lambda a_, s_, n_, v_: _nt(a_, s_) + _bdot(n_, v_), a_s, sbd, n_ak, v_s)
    ys0 = each(lambda r_, s_, n_, v_: _nt(r_, s_) + _bdot(n_, v_), r_s, sbd, n_rk, v_s)
    inv = each(lambda n_: eye + n_, n_ab)
    pw = n_ab
    for _ in range(max(0, math.ceil(math.log2(c)) - 1)):
        pw = each(lambda x: _bdot(x, x), pw)
        inv = each(lambda x, q: x + _bdot(x, q), inv, pw)
    u = each(_bdot, inv, z)
    ys = each(lambda y0, n_, u_: y0 + _bdot(n_, u_), ys0, n_rb, u)
    s_new = each(lambda s_, u_, b_, v_, k_, p_: (s_ + _tn(u_, b_) + _tn(v_, k_)) * p_[c - 1:c, :],
                 sbd, u, b_s, v_s, k_s, pin)
    y = each(lambda x: x[:c] + x[c:], ys)
    mean = each(lambda x: _dot_exact_rhs(x, seg) * (1.0 / DH), y)
    dlt = each(lambda x, m_: x - m_, y, mean)
    var = each(lambda x: _dot_exact_rhs(x * x, seg) * (1.0 / DH), dlt)
    for i, (k, p) in enumerate(chains):
        s_ref[k, p] = s_new[i]
        ln = slice(p * LANE, (p + 1) * LANE)
        yn = dlt[i] * lax.rsqrt(var[i] + GN_EPS) * lnw_ref[:, ln] + lnb_ref[:, ln]
        o_ref[k, :, ln] = (yn + bonus[i]) * g[i]

    @pl.when(ci == pl.num_programs(1) - 1)
    def _():
        so_ref[...] = s_ref[...]


def _rwkv_scan(packed, s0_bd, ln_w, ln_b, nb, tn, c, nbb):
    nc = tn // c
    assert nb % nbb == 0
    xspec = lambda k: pl.BlockSpec((c, PACK_W), lambda gb, ci, k=k: ((gb * nbb + k) * nc + ci, 0))
    return pl.pallas_call(
        functools.partial(_rwkv_scan_kernel, c=c, nbb=nbb),
        grid=(nb // nbb, nc),
        in_specs=[xspec(k) for k in range(nbb)] + [
            pl.BlockSpec((nbb, N_PAIR, LANE, LANE), lambda gb, ci: (gb, 0, 0, 0)),
            pl.BlockSpec((1, HW), lambda gb, ci: (0, 0)),
            pl.BlockSpec((1, HW), lambda gb, ci: (0, 0))],
        out_specs=[pl.BlockSpec((nbb, c, HW), lambda gb, ci: (gb, ci, 0)),
                   pl.BlockSpec((nbb, N_PAIR, LANE, LANE), lambda gb, ci: (gb, 0, 0, 0))],
        out_shape=[jax.ShapeDtypeStruct((nb, tn, HW), F32),
                   jax.ShapeDtypeStruct((nb, N_PAIR, LANE, LANE), F32)],
        scratch_shapes=[pltpu.VMEM((nbb, N_PAIR, LANE, LANE), F32)],
        compiler_params=_cparams("arbitrary", "arbitrary"),
        name="rwkv_scan",
    )(*([packed] * nbb), s0_bd, ln_w, ln_b)


def _pair_states(s):
    b = s.shape[0]
    s = s.reshape(b, N_PAIR, 2, DH, DH)
    z = jnp.zeros_like(s[:, :, 0])
    top = jnp.concatenate([s[:, :, 0], z], axis=-1)
    bot = jnp.concatenate([z, s[:, :, 1]], axis=-1)
    return jnp.concatenate([top, bot], axis=-2)


def _unpair_states(sbd):
    b = sbd.shape[0]
    return jnp.stack([sbd[:, :, :DH, :DH], sbd[:, :, DH:, DH:]], axis=2).reshape(b, NH, DH, DH)


def _outproj_kernel(x_ref, of_ref, or_ref, gt_ref, w_ref, o_ref, *, lay, tm):
    i = pl.program_id(0)
    (gate,) = _mod_params(lay, tm, i, (gt_ref,))
    y = (jnp.dot(of_ref[...].astype(BF16), w_ref[0:HW, :], preferred_element_type=F32)
         + jnp.dot(or_ref[...].astype(BF16), w_ref[HW:2 * HW, :], preferred_element_type=F32))
    o_ref[...] = x_ref[...] + gate * y


def _outproj(lay, x_all, o_fox, o_rwkv, mod, li, w_out):
    tm = lay.tm
    nt, _, _ = _tile_rows(lay, tm)
    d = D_MODEL
    return pl.pallas_call(
        functools.partial(_outproj_kernel, lay=lay, tm=tm),
        grid=(nt,),
        in_specs=[pl.BlockSpec((tm, d), lambda i: (i, 0)),
                  pl.BlockSpec((tm, HW), lambda i: (i, 0)),
                  pl.BlockSpec((tm, HW), lambda i: (i, 0)),
                  _mod_spec(lay, li, 2, 1),
                  pl.BlockSpec((2 * HW, d), lambda i: (0, 0))],
        out_specs=pl.BlockSpec((tm, d), lambda i: (i, 0)),
        out_shape=jax.ShapeDtypeStruct((lay.mp, d), F32),
        compiler_params=_cparams("arbitrary"),
        name="outproj",
    )(x_all, o_fox, o_rwkv, mod, w_out)


def _ffn_kernel(x_ref, g_ref, sh_ref, sc_ref, gt_ref, wg_ref, wu_ref, wd_ref, o_ref, h_ref, acc_ref, *, lay, tm):
    i = pl.program_id(0)
    f = pl.program_id(1)

    @pl.when(f == 0)
    def _():
        sh, sc = _mod_params(lay, tm, i, (sh_ref, sc_ref))
        h_ref[...] = (_rms(x_ref[...], g_ref[...]) * (1.0 + sc) + sh).astype(BF16)
        acc_ref[...] = jnp.zeros_like(acc_ref)

    h = h_ref[...]
    a = jnp.dot(h, wg_ref[...].astype(BF16), preferred_element_type=F32)
    u = jnp.dot(h, wu_ref[...].astype(BF16), preferred_element_type=F32)
    hid = (a * _sigmoid(a) * u).astype(BF16)
    acc_ref[...] += jnp.dot(hid, wd_ref[...].astype(BF16), preferred_element_type=F32)

    @pl.when(f == pl.num_programs(1) - 1)
    def _():
        (gate,) = _mod_params(lay, tm, i, (gt_ref,))
        o_ref[...] = x_ref[...] + gate * acc_ref[...]


def _ffn(lay, x_all, mod, li, g, e, wg, wu, wd):
    tm = lay.tm
    nt, _, _ = _tile_rows(lay, tm)
    d = D_MODEL
    dff = wg.shape[2]
    tf = 256
    return pl.pallas_call(
        functools.partial(_ffn_kernel, lay=lay, tm=tm),
        grid=(nt, dff // tf),
        in_specs=[pl.BlockSpec((tm, d), lambda i, f: (i, 0)),
                  pl.BlockSpec((1, d), lambda i, f: (0, 0)),
                  _mod_spec(lay, li, 3, 2), _mod_spec(lay, li, 4, 2), _mod_spec(lay, li, 5, 2),
                  pl.BlockSpec((None, d, tf), lambda i, f: (e, 0, f)),
                  pl.BlockSpec((None, d, tf), lambda i, f: (e, 0, f)),
                  pl.BlockSpec((None, tf, d), lambda i, f: (e, f, 0))],
        out_specs=pl.BlockSpec((tm, d), lambda i, f: (i, 0)),
        out_shape=jax.ShapeDtypeStruct((lay.mp, d), F32),
        scratch_shapes=[pltpu.VMEM((tm, d), BF16), pltpu.VMEM((tm, d), F32)],
        compiler_params=_cparams("arbitrary", "arbitrary"),
        name="ffn",
    )(x_all, g, mod, mod, mod, wg, wu, wd)


def _pool_mix(pooled_minus_h, w_ref, ps_ref):
    pc = D_MODEL // len(POOL_WINDOWS)
    ys = [jnp.dot(pooled_minus_h[gi].astype(BF16), w_ref[gi], preferred_element_type=F32)
          for gi in range(len(POOL_WINDOWS))]
    return jnp.concatenate(ys, axis=-1) * ps_ref[...]


def _pool_prompt_kernel(x_ref, g_ref, sh_ref, sc_ref, gt_ref, w_ref, ps_ref, o_ref, hl_ref, carry_ref, *, lay, tm):
    i = pl.program_id(0)
    _, _, tpb = _tile_rows(lay, tm)
    sh, sc, gate = _mod_params(lay, tm, i, (sh_ref, sc_ref, gt_ref))
    h = _rms(x_ref[...], g_ref[...]) * (1.0 + sc) + sh

    @pl.when(i == 0)
    def _():
        carry_ref[...] = jnp.zeros_like(carry_ref)

    first = (i % tpb) == 0
    halo = jnp.where(first, 0.0, carry_ref[...])
    carry_ref[...] = h[tm - HALO:, :]
    hl_ref[...] = h[tm - HALO:, :]
    ext = jnp.concatenate([halo, h], axis=0)
    pos = (i % tpb) * tm + _iota((tm, 1), 0)
    pc = D_MODEL // len(POOL_WINDOWS)
    zs = []
    for gi, win in enumerate(POOL_WINDOWS):
        s = ext[:, gi * pc:(gi + 1) * pc]
        step = 1
        while step < win:
            s = s + pltpu.roll(s, step, 0)
            step *= 2
        cnt = jnp.minimum(pos + 1, win).astype(F32)
        zs.append(s[HALO:, :] / cnt - h[:, gi * pc:(gi + 1) * pc])
    o_ref[...] = x_ref[...] + gate * _pool_mix(zs, w_ref, ps_ref)


def _pool_prompt(lay, x_all, mod, li, g, w_pool, pool_scale):
    tm = lay.tm
    _, n_pt, tpb = _tile_rows(lay, tm)
    d = D_MODEL
    pc = d // len(POOL_WINDOWS)
    return pl.pallas_call(
        functools.partial(_pool_prompt_kernel, lay=lay, tm=tm),
        grid=(n_pt,),
        in_specs=[pl.BlockSpec((tm, d), lambda i: (i, 0)),
                  pl.BlockSpec((1, d), lambda i: (0, 0)),
                  _mod_spec(lay, li, 0, 1), _mod_spec(lay, li, 1, 1), _mod_spec(lay, li, 2, 1),
                  pl.BlockSpec((len(POOL_WINDOWS), pc, pc), lambda i: (0, 0, 0)),
                  pl.BlockSpec((1, d), lambda i: (0, 0))],
        out_specs=[pl.BlockSpec((tm, d), lambda i: (i, 0)),
                   pl.BlockSpec((None, HALO, d), lambda i: (i // tpb, 0, 0))],
        out_shape=[jax.ShapeDtypeStruct((lay.mp, d), F32),
                   jax.ShapeDtypeStruct((lay.bp, HALO, d), F32)],
        scratch_shapes=[pltpu.VMEM((HALO, d), F32)],
        input_output_aliases={0: 0},
        compiler_params=_cparams("arbitrary"),
        name="pool_prompt",
    )(x_all, g, mod, mod, mod, w_pool, pool_scale)


def _pool_sample_kernel(x_ref, buf_ref, g_ref, sh_ref, sc_ref, gt_ref, w_ref, ps_ref, o_ref, h_ref, *, lay):
    bs, ts = lay.bs, lay.ts
    sl = slice(lay.bp_pad, lay.bp_pad + bs)
    sh, sc, gate = sh_ref[sl, :], sc_ref[sl, :], gt_ref[sl, :]
    pc = D_MODEL // len(POOL_WINDOWS)
    hs = []
    for t in range(ts):
        hs.append(_rms(x_ref[t * bs:(t + 1) * bs, :], g_ref[...]) * (1.0 + sc) + sh)
        h_ref[t * bs:(t + 1) * bs, :] = hs[t]
    ext = [buf_ref[e] for e in range(POOL_BUF)] + hs
    for t in range(ts):
        zs = []
        for gi, win in enumerate(POOL_WINDOWS):
            cs = slice(gi * pc, (gi + 1) * pc)
            s = ext[POOL_BUF + t][:, cs]
            for j in range(1, win):
                s = s + ext[POOL_BUF + t - j][:, cs]
            zs.append(s * (1.0 / win) - hs[t][:, cs])
        o_ref[t * bs:(t + 1) * bs, :] = x_ref[t * bs:(t + 1) * bs, :] + gate * _pool_mix(zs, w_ref, ps_ref)


def _pool_sample(lay, x_all, buf_t, mod, li, g, w_pool, pool_scale):
    d = D_MODEL
    pc = d // len(POOL_WINDOWS)
    sblk = lay.np_rows // lay.ms
    mspec = lambda k: pl.BlockSpec((None, lay.mc, d), lambda i: (li, 0, k))
    return pl.pallas_call(
        functools.partial(_pool_sample_kernel, lay=lay),
        grid=(1,),
        in_specs=[pl.BlockSpec((lay.ms, d), lambda i: (sblk, 0)),
                  pl.BlockSpec((POOL_BUF, lay.bs, d), lambda i: (0, 0, 0)),
                  pl.BlockSpec((1, d), lambda i: (0, 0)),
                  mspec(0), mspec(1), mspec(2),
                  pl.BlockSpec((len(POOL_WINDOWS), pc, pc), lambda i: (0, 0, 0)),
                  pl.BlockSpec((1, d), lambda i: (0, 0))],
        out_specs=[pl.BlockSpec((lay.ms, d), lambda i: (sblk, 0)),
                   pl.BlockSpec((lay.ms, d), lambda i: (0, 0))],
        out_shape=[jax.ShapeDtypeStruct((lay.mp, d), F32),
                   jax.ShapeDtypeStruct((lay.ms, d), F32)],
        input_output_aliases={0: 0},
        compiler_params=_cparams("arbitrary"),
        name="pool_sample",
    )(x_all, buf_t, g, mod, mod, mod, w_pool, pool_scale)


def _router_kernel(x_ref, g_ref, sh_ref, sc_ref, rw_ref, rb_ref, h_ref, meta_ref, *, lay, tm):
    i = pl.program_id(0)
    sh, sc = _mod_params(lay, tm, i, (sh_ref, sc_ref))
    h = _rms(x_ref[...], g_ref[...]) * (1.0 + sc) + sh
    h_ref[...] = h
    lane = _iota((tm, LANE), 1)
    logits = jnp.where(lane < N_EXPERTS, _dot3(h, rw_ref[...]) + rb_ref[...], NEG_INF)
    m1 = jnp.max(logits, axis=-1, keepdims=True)
    i1 = jnp.min(jnp.where(logits == m1, lane, LANE), axis=-1, keepdims=True)
    rest = jnp.where(lane == i1, NEG_INF, logits)
    m2 = jnp.max(rest, axis=-1, keepdims=True)
    i2 = jnp.min(jnp.where(rest == m2, lane, LANE), axis=-1, keepdims=True)
    e2 = jnp.exp(m2 - m1)
    g1 = 1.0 / (1.0 + e2)
    g2 = e2 / (1.0 + e2)
    meta = jnp.where(lane == 0, i1.astype(F32), 0.0)
    meta = jnp.where(lane == 1, i2.astype(F32), meta)
    meta = jnp.where(lane == 2, g1, meta)
    meta_ref[...] = jnp.where(lane == 3, g2, meta)


def _router(lay, x_all, mod, li, g, rw_pad, rb_pad):
    tm = min(512, lay.tm)
    nt, _, _ = _tile_rows(lay, tm)
    d = D_MODEL
    return pl.pallas_call(
        functools.partial(_router_kernel, lay=lay, tm=tm),
        grid=(nt,),
        in_specs=[pl.BlockSpec((tm, d), lambda i: (i, 0)),
                  pl.BlockSpec((1, d), lambda i: (0, 0)),
                  _mod_spec(lay, li, 3, 1), _mod_spec(lay, li, 4, 1),
                  pl.BlockSpec((d, LANE), lambda i: (0, 0)),
                  pl.BlockSpec((1, LANE), lambda i: (0, 0))],
        out_specs=[pl.BlockSpec((tm, d), lambda i: (i, 0)),
                   pl.BlockSpec((tm, LANE), lambda i: (i, 0))],
        out_shape=[jax.ShapeDtypeStruct((lay.mp, d), F32),
                   jax.ShapeDtypeStruct((lay.mp, LANE), F32)],
        compiler_params=_cparams("arbitrary"),
        name="router",
    )(x_all, g, mod, mod, rw_pad, rb_pad)


GATHER_ROWS = 256


def _gather_kernel(idx_ref, nv_ref, src_ref, o_ref, sem):
    base = pl.program_id(0) * GATHER_ROWS

    def row_copy(r):
        return pltpu.make_async_copy(src_ref.at[pl.ds(idx_ref[base + r], 1)], o_ref.at[pl.ds(r, 1)], sem)

    def start(r, c):
        row_copy(r).start()
        return c

    def wait(r, c):
        row_copy(r).wait()
        return c

    @pl.when(base < nv_ref[0])
    def _():
        lax.fori_loop(0, GATHER_ROWS, start, 0, unroll=8)
        lax.fori_loop(0, GATHER_ROWS, wait, 0, unroll=8)

    @pl.when(base >= nv_ref[0])
    def _():
        o_ref[...] = jnp.zeros_like(o_ref)


def _gather_rows(src, idx, n_valid):
    n = idx.shape[0]
    assert n % GATHER_ROWS == 0
    d = src.shape[1]
    grid_spec = pltpu.PrefetchScalarGridSpec(
        num_scalar_prefetch=2, grid=(n // GATHER_ROWS,),
        in_specs=[pl.BlockSpec(memory_space=pl.ANY)],
        out_specs=pl.BlockSpec((GATHER_ROWS, d), lambda i, idx, nv: (i, 0)),
        scratch_shapes=[pltpu.SemaphoreType.DMA(())])
    return pl.pallas_call(
        _gather_kernel, grid_spec=grid_spec,
        out_shape=jax.ShapeDtypeStruct((n, d), src.dtype),
        compiler_params=pltpu.CompilerParams(dimension_semantics=("arbitrary",), vmem_limit_bytes=VMEM_LIMIT,
                                             disable_bounds_checks=True),
        name="gather_rows",
    )(idx, n_valid, src)


def _expert_kernel(te_ref, nu_ref, x_ref, wg_ref, wu_ref, wd_ref, o_ref, h_ref, acc_ref):
    i = pl.program_id(0)
    f = pl.program_id(1)
    used = i < nu_ref[0]

    @pl.when(jnp.logical_and(used, f == 0))
    def _():
        h_ref[...] = x_ref[...].astype(BF16)
        acc_ref[...] = jnp.zeros_like(acc_ref)

    @pl.when(used)
    def _():
        h = h_ref[...]
        a = jnp.dot(h, wg_ref[...].astype(BF16), preferred_element_type=F32)
        u = jnp.dot(h, wu_ref[...].astype(BF16), preferred_element_type=F32)
        hid = (a * _sigmoid(a) * u).astype(BF16)
        acc_ref[...] += jnp.dot(hid, wd_ref[...].astype(BF16), preferred_element_type=F32)

    @pl.when(f == pl.num_programs(1) - 1)
    def _():
        o_ref[...] = jnp.where(used, acc_ref[...], 0.0)


def _experts(xg, tile_expert, n_used, o, wg, wu, wd, tme):
    mg, d = xg.shape
    dff = wg.shape[3]
    tf = 512
    nf = dff // tf

    def widx(i, f, te, nu):
        return jnp.where(i < nu[0], f, nf - 1)

    grid_spec = pltpu.PrefetchScalarGridSpec(
        num_scalar_prefetch=2, grid=(mg // tme, nf),
        in_specs=[pl.BlockSpec((tme, d), lambda i, f, te, nu: (jnp.minimum(i, nu[0] - 1), 0)),
                  pl.BlockSpec((None, None, d, tf), lambda i, f, te, nu: (o, te[i], 0, widx(i, f, te, nu))),
                  pl.BlockSpec((None, None, d, tf), lambda i, f, te, nu: (o, te[i], 0, widx(i, f, te, nu))),
                  pl.BlockSpec((None, None, tf, d), lambda i, f, te, nu: (o, te[i], widx(i, f, te, nu), 0))],
        out_specs=pl.BlockSpec((tme, d), lambda i, f, te, nu: (i, 0)),
        scratch_shapes=[pltpu.VMEM((tme, d), BF16), pltpu.VMEM((tme, d), F32)])
    return pl.pallas_call(
        _expert_kernel, grid_spec=grid_spec,
        out_shape=jax.ShapeDtypeStruct((mg, d), F32),
        compiler_params=_cparams("arbitrary", "arbitrary"),
        name="experts",
    )(tile_expert, n_used, xg, wg, wu, wd)


def _combine_kernel(x_ref, y1_ref, y2_ref, meta_ref, gt_ref, o_ref, *, lay, tm):
    i = pl.program_id(0)
    (gate,) = _mod_params(lay, tm, i, (gt_ref,))
    meta = meta_ref[...]
    y = meta[:, 2:3] * y1_ref[...] + meta[:, 3:4] * y2_ref[...]
    o_ref[...] = x_ref[...] + gate * y


def _combine(lay, x_all, y1, y2, meta, mod, li):
    tm = lay.tm
    nt, _, _ = _tile_rows(lay, tm)
    d = D_MODEL
    return pl.pallas_call(
        functools.partial(_combine_kernel, lay=lay, tm=tm),
        grid=(nt,),
        in_specs=[pl.BlockSpec((tm, d), lambda i: (i, 0)),
                  pl.BlockSpec((tm, d), lambda i: (i, 0)),
                  pl.BlockSpec((tm, d), lambda i: (i, 0)),
                  pl.BlockSpec((tm, LANE), lambda i: (i, 0)),
                  _mod_spec(lay, li, 5, 1)],
        out_specs=pl.BlockSpec((tm, d), lambda i: (i, 0)),
        out_shape=jax.ShapeDtypeStruct((lay.mp, d), F32),
        compiler_params=_cparams("arbitrary"),
        name="combine",
    )(x_all, y1, y2, meta, mod)


def _moe(lay, x_all, mod, li, g, router_w, router_b, o, wg, wu, wd):
    d = D_MODEL
    rw_pad = jnp.pad(router_w, ((0, 0), (0, LANE - N_EXPERTS)))
    rb_pad = jnp.pad(router_b, (0, LANE - N_EXPERTS)).reshape(1, LANE)
    h2, meta = _router(lay, x_all, mod, li, g, rw_pad, rb_pad)

    ntok = lay.ntok
    tme = lay.tm
    ex = meta[:ntok, 0:2].astype(jnp.int32).reshape(-1)
    onehot = (ex[:, None] == jnp.arange(N_EXPERTS)[None, :]).astype(jnp.int32)
    rank = jnp.sum((jnp.cumsum(onehot, axis=0) - onehot) * onehot, axis=1)
    counts = jnp.sum(onehot, axis=0)
    tiles = (counts + tme - 1) // tme
    tile_end = jnp.cumsum(tiles)
    offs = (tile_end - tiles) * tme
    dest = offs[ex] + rank
    n_tiles = (2 * ntok + N_EXPERTS * (tme - 1)) // tme + 1
    mg = n_tiles * tme
    src = jnp.zeros((mg,), jnp.int32).at[dest].set(jnp.arange(2 * ntok, dtype=jnp.int32) // 2)
    tile_expert = jnp.minimum(jnp.sum(jnp.arange(n_tiles)[:, None] >= tile_end[None, :], axis=1),
                              N_EXPERTS - 1).astype(jnp.int32)
    n_used = tile_end[-1:].astype(jnp.int32)
    last_e = tile_expert[jnp.maximum(n_used[0] - 1, 0)]
    tile_expert = jnp.where(jnp.arange(n_tiles) < n_used[0], tile_expert, last_e)

    xg = _gather_rows(h2, src, n_used * tme)
    yg = _experts(xg, tile_expert, n_used, o, wg, wu, wd, tme)
    dest2 = jnp.pad(dest.reshape(ntok, 2), ((0, lay.mp - ntok), (0, 0)))
    all_rows = jnp.full((1,), lay.mp, jnp.int32)
    y1 = _gather_rows(yg, dest2[:, 0], all_rows)
    y2 = _gather_rows(yg, dest2[:, 1], all_rows)
    return _combine(lay, x_all, y1, y2, meta, mod, li)


def _perm_rwkv_cols(a):
    r, wl, k, v, al, gl = jnp.split(a, [HW, HW + LORA_W, 2 * HW + LORA_W, 3 * HW + LORA_W, 3 * HW + LORA_W + LORA_A],
                                    axis=-1)
    return jnp.concatenate([r, k, v, wl, al, gl], axis=-1)


def _unperm_rwkv_cols(a):
    r, k, v, wl, al, gl = jnp.split(a, [HW, 2 * HW, 3 * HW, 3 * HW + LORA_W, 3 * HW + LORA_W + LORA_A], axis=-1)
    return jnp.concatenate([r, wl, k, v, al, gl], axis=-1)


def kernel(x_prompt, x_sample, c_prompt, c_sample, cache_fox_k, cache_fox_v, cache_fox_logf, page_table, state_rwkv, state_rwkv_shift, state_pool, norm1_g, norm2_g, ada_w, ada_b, w_in, w_out, fox_q_gain, fox_k_gain, fox_f_bias, rwkv_mu, rwkv_w0, rwkv_w2, rwkv_a0, rwkv_a2, rwkv_g2, rwkv_k_k, rwkv_k_a, rwkv_r_k, rwkv_ln_w, rwkv_ln_b, ffn_w_gate, ffn_w_up, ffn_w_down, pool_w, pool_scale, moe_router_w, moe_router_b, moe_w_gate, moe_w_up, moe_w_down):
    bp, t, d = x_prompt.shape
    bs, ts, _ = x_sample.shape
    assert d == D_MODEL
    lay = _Layout(bp, t, bs, ts)
    depth = ada_w.shape[0]
    n_phys = cache_fox_k.shape[1]
    fox_in = 3 * HW + NH

    xs_tb = jnp.swapaxes(x_sample, 0, 1).reshape(lay.ms, d)
    x_all = jnp.concatenate([x_prompt.reshape(lay.np_rows, d), xs_tb, jnp.zeros((lay.mp - lay.ntok, d), F32)], axis=0)
    c_all = jnp.concatenate([c_prompt, jnp.zeros((lay.bp_pad - bp, d), F32), c_sample], axis=0)
    mod = _ada(c_all, ada_w, ada_b)

    slot_minor = lambda c: jnp.transpose(c, (0, 1, 3, 4, 2)).reshape(c.shape[0], n_phys, HW, PAGE)
    cache_kt = slot_minor(cache_fox_k)
    cache_vt = slot_minor(cache_fox_v)
    cache_lft = jnp.swapaxes(cache_fox_logf, 2, 3)
    head_of_lane = jnp.arange(HW) // DH
    qmask = (jnp.arange(NH)[:, None] == head_of_lane[None, :]).astype(F32)
    tpad = 8

    outs = {k: [] for k in ("kp", "ks", "vp", "vs", "fp", "fs", "sp", "ss", "shp", "shs", "pp", "ps")}
    for li in range(depth):
        g1 = norm1_g[li].reshape(1, d)
        g2 = norm2_g[li].reshape(1, d)
        if li % 2 == 0:
            e = li // 2
            wi = w_in[e]
            w_cat = jnp.concatenate(
                [_perm_rwkv_cols(wi[:, fox_in:]), wi[:, :3 * HW],
                 jnp.pad(wi[:, 3 * HW:fox_in], ((0, 0), (0, F_PAD - NH)))], axis=1).astype(BF16)
            qk_gain = jnp.concatenate([jnp.tile(fox_q_gain[e], NH), jnp.tile(fox_k_gain[e], NH)]).reshape(1, 2 * HW)
            f_bias = jnp.pad(fox_f_bias[e], (0, F_PAD - NH)).reshape(1, F_PAD)
            proj = _inproj(lay, x_all, mod, li, g1, w_cat, qk_gain, f_bias)

            fq, fk = _cumf(lay, proj)
            fk_rows = jnp.swapaxes(fk[:, :NH].reshape(bp, t, NH), 1, 2).reshape(bp, N_PAIR, 2, t)
            o_fox_p = _fox_prompt(lay, proj, fq, fk_rows)

            ps_rows = proj[lay.np_rows:lay.ntok].reshape(ts, bs, PW).swapaxes(0, 1)
            q_s = ps_rows[..., COL_Q:COL_K] * (DH ** -0.5)
            k_s = ps_rows[..., COL_K:COL_V]
            v_s = ps_rows[..., COL_V:COL_F]
            lf_s = ps_rows[..., COL_F:COL_F + NH]
            qx = (q_s[:, :, None, :] * qmask[None, None, :, :]).reshape(bs, ts * NH, HW)
            padt = lambda a: jnp.pad(a, ((0, 0), (0, tpad - ts), (0, 0)))
            o_fox_s = _fox_sample(lay, e, page_table, qx, jnp.swapaxes(padt(k_s), 1, 2), padt(v_s),
                                  jnp.swapaxes(padt(lf_s), 1, 2), cache_kt, cache_vt, cache_lft)
            o_fox = jnp.concatenate([o_fox_p, jnp.swapaxes(o_fox_s, 0, 1).reshape(lay.ms, HW),
                                     jnp.zeros((lay.mp - lay.ntok, HW), F32)], axis=0)

            pad_lora = lambda w_, top: jnp.pad(w_, ((0, LANE - w_.shape[0]), (0, 0)) if top else
                                               ((LANE - w_.shape[0], 0), (0, 0))).astype(BF16)
            rw = dict(mu=_perm_rwkv_cols(rwkv_mu[e]).reshape(1, RW_IN), w0=rwkv_w0[e].reshape(1, HW),
                      w2=pad_lora(rwkv_w2[e], True), a0=rwkv_a0[e].reshape(1, HW), a2=pad_lora(rwkv_a2[e], False),
                      g2=rwkv_g2[e].astype(BF16), k_k=rwkv_k_k[e].reshape(1, HW), k_a=rwkv_k_a[e].reshape(1, HW),
                      r_k=rwkv_r_k[e].reshape(1, HW))
            packed = _rwkv_pre(lay, proj, _perm_rwkv_cols(state_rwkv_shift[e]), rw)
            ln_w = rwkv_ln_w[e].reshape(1, HW)
            ln_b = rwkv_ln_b[e].reshape(1, HW)
            cp = min(64, t)
            o_rw_p, s_p = _rwkv_scan(packed, jnp.zeros((bp, N_PAIR, LANE, LANE), F32), ln_w, ln_b,
                                     bp, t, cp, 2 if bp % 2 == 0 else 1)
            pk_s = packed[lay.np_rows:lay.ntok].reshape(ts, bs, PACK_W).swapaxes(0, 1)
            pk_s = jnp.pad(pk_s, ((0, 0), (0, tpad - ts), (0, 0))).reshape(bs * tpad, PACK_W)
            o_rw_s, s_s = _rwkv_scan(pk_s, _pair_states(state_rwkv[e]), ln_w, ln_b, bs, tpad, tpad,
                                     4 if bs % 4 == 0 else 1)
            o_rw_s = o_rw_s[:, :ts].swapaxes(0, 1).reshape(lay.ms, HW)
            o_rwkv = jnp.concatenate([o_rw_p.reshape(lay.np_rows, HW), o_rw_s,
                                      jnp.zeros((lay.mp - lay.ntok, HW), F32)], axis=0)

            x_all = _outproj(lay, x_all, o_fox, o_rwkv, mod, li, w_out[e].astype(BF16))
            x_all = _ffn(lay, x_all, mod, li, g2, e, ffn_w_gate, ffn_w_up, ffn_w_down)

            outs["kp"].append(proj[:lay.np_rows, COL_K:COL_V].reshape(bp, t, NH, DH))
            outs["vp"].append(proj[:lay.np_rows, COL_V:COL_F].reshape(bp, t, NH, DH))
            outs["fp"].append(proj[:lay.np_rows, COL_F:COL_F + NH].reshape(bp, t, NH))
            outs["ks"].append(k_s.reshape(bs, ts, NH, DH))
            outs["vs"].append(v_s.reshape(bs, ts, NH, DH))
            outs["fs"].append(lf_s)
            outs["sp"].append(_unpair_states(s_p))
            outs["ss"].append(_unpair_states(s_s))
            last_p = proj[t - 1:lay.np_rows:t, :RW_IN]
            last_s = proj[lay.np_rows + (ts - 1) * bs:lay.ntok, :RW_IN]
            outs["shp"].append(_unperm_rwkv_cols(last_p))
            outs["shs"].append(_unperm_rwkv_cols(last_s))
        else:
            o = li // 2
            w_pool = pool_w[o].astype(BF16)
            p_scale = pool_scale[o].reshape(1, d)
            x_all, hl = _pool_prompt(lay, x_all, mod, li, g1, w_pool, p_scale)
            buf_t = jnp.swapaxes(state_pool[o], 0, 1)
            x_all, h_s = _pool_sample(lay, x_all, buf_t, mod, li, g1, w_pool, p_scale)
            outs["pp"].append(hl[:, HALO - POOL_BUF:, :])
            h_s_bt = jnp.swapaxes(h_s.reshape(ts, bs, d), 0, 1)
            outs["ps"].append(jnp.concatenate([state_pool[o], h_s_bt], axis=1)[:, -POOL_BUF:])
            x_all = _moe(lay, x_all, mod, li, g2, moe_router_w[o], moe_router_b[o],
                         o, moe_w_gate, moe_w_up, moe_w_down)

    y_p = x_all[:lay.np_rows].reshape(bp, t, d)
    y_s = jnp.swapaxes(x_all[lay.np_rows:lay.ntok].reshape(ts, bs, d), 0, 1)
    st = lambda k: jnp.stack(outs[k])
    return (y_p, y_s, st("kp"), st("ks"), st("vp"), st("vs"), st("fp"), st("fs"), st("sp"), st("ss"),
            st("shp"), st("shs"), st("pp"), st("ps"))
```

```python
import functools
import math

import jax
import jax.numpy as jnp
from jax import lax
from jax.experimental import pallas as pl
from jax.experimental.pallas import tpu as pltpu

F32 = jnp.float32
BF16 = jnp.bfloat16

NH = 8
DH = 64
HW = NH * DH
N_PAIR = NH // 2
LANE = 128
LORA_W = 64
LORA_A = 64
LORA_G = 128
RW_IN = 3 * HW + LORA_W + LORA_A + LORA_G
F_PAD = 256
PW = RW_IN + 3 * HW + F_PAD
COL_Q = RW_IN
COL_K = RW_IN + HW
COL_V = RW_IN + 2 * HW
COL_F = RW_IN + 3 * HW
PACK_W = 8 * HW
POOL_WINDOWS = (2, 4, 8, 16)
POOL_BUF = 15
HALO = 16
N_EXPERTS = 8
RMS_EPS = 1e-6
GN_EPS = 64e-5
NEG_INF = -1e30
PAGE = 128
VMEM_LIMIT = 56 * 1024 * 1024


def _cparams(*sem):
    return pltpu.CompilerParams(dimension_semantics=sem, vmem_limit_bytes=VMEM_LIMIT)


def _bdot(a, b):
    return jnp.dot(a.astype(BF16), b.astype(BF16), preferred_element_type=F32)


def _nt(a, b):
    return lax.dot_general(a.astype(BF16), b.astype(BF16), (((1,), (1,)), ((), ())),
                           preferred_element_type=F32)


def _tn(a, b):
    return lax.dot_general(a.astype(BF16), b.astype(BF16), (((0,), (0,)), ((), ())),
                           preferred_element_type=F32)


def _split(x, terms):
    out = []
    for _ in range(terms - 1):
        h = x.astype(BF16)
        out.append(h)
        x = x - h.astype(F32)
    out.append(x.astype(BF16))
    return out


def _dot_exact_lhs(a01, x, terms=3):
    a = a01.astype(BF16)
    acc = None
    for t in _split(x, terms):
        d = jnp.dot(a, t, preferred_element_type=F32)
        acc = d if acc is None else acc + d
    return acc


def _dot_exact_rhs(x, b01, terms=2):
    b = b01.astype(BF16)
    acc = None
    for t in _split(x, terms):
        d = jnp.dot(t, b, preferred_element_type=F32)
        acc = d if acc is None else acc + d
    return acc


def _dot3(a, b):
    ah, al = _split(a, 2)
    bh, bl = _split(b, 2)
    return (jnp.dot(ah, bh, preferred_element_type=F32) + jnp.dot(ah, bl, preferred_element_type=F32)
            + jnp.dot(al, bh, preferred_element_type=F32))


def _sigmoid(x):
    return 1.0 / (1.0 + jnp.exp(-x))


def _softplus(x):
    return jnp.maximum(x, 0.0) + jnp.log(1.0 + jnp.exp(-jnp.abs(x)))


def _iota(shape, dim):
    return lax.broadcasted_iota(jnp.int32, shape, dim)


def _seg_ones(n, seg):
    return (_iota((n, n), 0) // seg == _iota((n, n), 1) // seg).astype(BF16)


class _Layout:
    def __init__(self, bp, t, bs, ts):
        self.bp, self.t, self.bs, self.ts = bp, t, bs, ts
        self.np_rows = bp * t
        self.ms = bs * ts
        self.tm = min(1024, t)
        assert t % self.tm == 0 and self.tm % bs == 0 and self.ms <= self.tm
        assert self.np_rows % self.ms == 0 and bs % 8 == 0
        self.mp = self.np_rows + self.tm
        self.ntok = self.np_rows + self.ms
        self.bp_pad = 8 * ((bp + 7) // 8)
        self.mc = self.bp_pad + bs


def _tile_rows(lay, tm):
    assert lay.tm % tm == 0 and tm % lay.bs == 0
    return lay.mp // tm, lay.np_rows // tm, lay.t // tm


def _mod_params(lay, tm, i, refs):
    _, n_pt, tpb = _tile_rows(lay, tm)
    is_prompt = i < n_pt
    b = jnp.minimum(i // tpb, lay.bp - 1)
    out = []
    for r in refs:
        p_row = r[pl.ds(b, 1), :]
        s_blk = r[lay.bp_pad:lay.bp_pad + lay.bs, :]
        s_rows = jnp.concatenate([s_blk] * (tm // lay.bs), axis=0)
        out.append(jnp.where(is_prompt, p_row, s_rows))
    return out


def _rms(x, g):
    return x * lax.rsqrt(jnp.mean(x * x, axis=-1, keepdims=True) + RMS_EPS) * g


def _mod_spec(lay, li, k, nidx):
    if nidx == 1:
        return pl.BlockSpec((None, lay.mc, D_MODEL), lambda i: (li, 0, k))
    return pl.BlockSpec((None, lay.mc, D_MODEL), lambda i, j: (li, 0, k))


D_MODEL = 1024


def _ada_kernel(c_ref, w_ref, b_ref, o_ref):
    c = c_ref[...]
    s = c * _sigmoid(c)
    o_ref[...] = _dot3(s, w_ref[...]) + b_ref[...]


def _ada(c_all, ada_w, ada_b):
    nl, d, n = ada_w.shape
    mc = c_all.shape[0]
    tn = 512
    return pl.pallas_call(
        _ada_kernel,
        grid=(nl, n // tn),
        in_specs=[pl.BlockSpec((mc, d), lambda l, j: (0, 0)),
                  pl.BlockSpec((None, d, tn), lambda l, j: (l, 0, j)),
                  pl.BlockSpec((None, 1, tn), lambda l, j: (l, 0, j))],
        out_specs=pl.BlockSpec((None, mc, tn), lambda l, j: (l, 0, j)),
        out_shape=jax.ShapeDtypeStruct((nl, mc, n), F32),
        compiler_params=_cparams("arbitrary", "arbitrary"),
        name="ada",
    )(c_all, ada_w, ada_b.reshape(nl, 1, n))


def _inproj_kernel(x_ref, g_ref, sh_ref, sc_ref, w_ref, qkg_ref, fb_ref, o_ref, *, lay, tm):
    i = pl.program_id(0)
    sh, sc = _mod_params(lay, tm, i, (sh_ref, sc_ref))
    h = (_rms(x_ref[...], g_ref[...]) * (1.0 + sc) + sh).astype(BF16)
    ch = 256
    seg = _seg_ones(ch, DH)
    for c in range(PW // ch):
        c0 = c * ch
        acc = jnp.dot(h, w_ref[:, c0:c0 + ch], preferred_element_type=F32)
        if COL_Q <= c0 < COL_V:
            ss = _dot_exact_rhs(acc * acc, seg)
            acc = acc * lax.rsqrt(ss * (1.0 / DH) + RMS_EPS) * qkg_ref[:, c0 - COL_Q:c0 - COL_Q + ch]
        elif c0 >= COL_F:
            z = acc + fb_ref[...]
            acc = -_softplus(-z)
        o_ref[:, c0:c0 + ch] = acc


def _inproj(lay, x_all, mod, li, g, w_cat, qk_gain, f_bias):
    tm = min(512, lay.tm)
    nt, _, _ = _tile_rows(lay, tm)
    d = D_MODEL
    return pl.pallas_call(
        functools.partial(_inproj_kernel, lay=lay, tm=tm),
        grid=(nt,),
        in_specs=[pl.BlockSpec((tm, d), lambda i: (i, 0)),
                  pl.BlockSpec((1, d), lambda i: (0, 0)),
                  _mod_spec(lay, li, 0, 1), _mod_spec(lay, li, 1, 1),
                  pl.BlockSpec((d, PW), lambda i: (0, 0)),
                  pl.BlockSpec((1, 2 * HW), lambda i: (0, 0)),
                  pl.BlockSpec((1, F_PAD), lambda i: (0, 0))],
        out_specs=pl.BlockSpec((tm, PW), lambda i: (i, 0)),
        out_shape=jax.ShapeDtypeStruct((lay.mp, PW), F32),
        compiler_params=_cparams("arbitrary"),
        name="inproj",
    )(x_all, g, mod, mod, w_cat, qk_gain, f_bias)


def _cumf_kernel(lf_ref, fq_ref, carry_ref, *, tc):
    c = pl.program_id(1)

    @pl.when(c == 0)
    def _():
        carry_ref[...] = jnp.zeros_like(carry_ref)

    tri = _iota((tc, tc), 0) >= _iota((tc, tc), 1)
    f = _dot_exact_lhs(tri, lf_ref[...]) + carry_ref[...]
    carry_ref[...] = f[tc - 1:tc, :]
    expand = _iota((LANE, HW), 0) == _iota((LANE, HW), 1) // DH
    fq_ref[...] = _dot_exact_rhs(f, expand, terms=3)


def _cumf(lay, proj):
    tc = min(256, lay.t)
    npc = lay.t // tc
    return pl.pallas_call(
        functools.partial(_cumf_kernel, tc=tc),
        grid=(lay.bp, npc),
        in_specs=[pl.BlockSpec((tc, LANE), lambda b, c: (b * npc + c, COL_F // LANE))],
        out_specs=pl.BlockSpec((tc, HW), lambda b, c: (b * npc + c, 0)),
        out_shape=jax.ShapeDtypeStruct((lay.np_rows, HW), F32),
        scratch_shapes=[pltpu.VMEM((1, LANE), F32)],
        compiler_params=_cparams("arbitrary", "arbitrary"),
        name="cumf",
    )(proj)


N_BIAS = 3


def _bias_lanes(x, spare0, terms, ones_first):
    lane = _iota((1, LANE), 1)
    t_off, o_off = (N_BIAS, 0) if ones_first else (0, N_BIAS)
    for n, tval in enumerate(terms):
        x = jnp.where(lane == spare0 + t_off + n, tval, x)
    return jnp.where((lane >= spare0 + o_off) & (lane < spare0 + o_off + N_BIAS), 1.0, x)


def _split_f32(x):
    out = []
    for _ in range(N_BIAS - 1):
        h = x.astype(BF16).astype(F32)
        out.append(h)
        x = x - h
    out.append(x)
    return out


def _fox_prompt_kernel(q_ref, k_ref, v_ref, fq_ref, fs_ref, o_ref, kb_ref, vt_ref, *, tq, t):
    qi = pl.program_id(2)
    log2e = 1.4426950408889634
    lane_hi = _iota((1, LANE), 1) >= DH
    spare = (DH, 0)

    @pl.when(qi == 0)
    def _():
        for c0 in range(0, t, tq):
            kc = k_ref[c0:c0 + tq, :]
            fs = fs_ref[c0:c0 + tq, :] * log2e
            for hh in range(2):
                neg = [-x for x in _split_f32(fs[:, hh * DH:hh * DH + 1])]
                kb_ref[hh, c0:c0 + tq, :] = _bias_lanes(kc, spare[hh], neg, ones_first=False).astype(BF16)
            vt_ref[:, c0:c0 + tq] = v_ref[c0:c0 + tq, :].T.astype(BF16)

    q = q_ref[...] * (DH ** -0.5 * log2e)
    fq = fq_ref[...] * log2e
    qh = []
    for hh in range(2):
        qm = jnp.where(lane_hi, q, 0.0) if hh else jnp.where(lane_hi, 0.0, q)
        qh.append(_bias_lanes(qm, spare[hh], _split_f32(fq[:, hh * DH:hh * DH + 1]), ones_first=True).astype(BF16))
    key_le_query = _iota((tq, tq), 0) <= _iota((tq, tq), 1)

    def step(kj, carry, diagonal):
        k0 = pl.multiple_of(kj * tq, tq)
        out = []
        for hh in range(2):
            m, l, acc = carry[hh]
            st = lax.dot_general(kb_ref[hh, pl.ds(k0, tq), :], qh[hh], (((1,), (1,)), ((), ())),
                                 preferred_element_type=F32)
            if diagonal:
                st = jnp.where(key_le_query, st, NEG_INF)
            m_new = jnp.maximum(m, jnp.max(st, axis=0, keepdims=True))
            p = jnp.exp2(st - m_new)
            alpha = jnp.exp2(m - m_new)
            l = alpha * l + jnp.sum(p, axis=0, keepdims=True)
            acc = alpha * acc + jnp.dot(vt_ref[hh * DH:(hh + 1) * DH, pl.ds(k0, tq)], p.astype(BF16),
                                        preferred_element_type=F32)
            out.append((m_new, l, acc))
        return tuple(out)

    init = (jnp.full((1, tq), NEG_INF, F32), jnp.zeros((1, tq), F32), jnp.zeros((DH, tq), F32))
    carry = lax.fori_loop(0, qi, lambda kj, c: step(kj, c, False), (init, init))
    (_, l0, acc0), (_, l1, acc1) = step(qi, carry, True)
    o_ref[...] = jnp.concatenate([acc0 / l0, acc1 / l1], axis=0).T


def _fox_prompt(lay, proj, fq):
    tq = min(512, lay.t)
    nq = lay.t // tq
    blk = lambda col: pl.BlockSpec((tq, LANE), lambda b, p, i: (b * nq + i, col // LANE + p))
    seq = lambda col: pl.BlockSpec((lay.t, LANE), lambda b, p, i: (b, col // LANE + p))
    return pl.pallas_call(
        functools.partial(_fox_prompt_kernel, tq=tq, t=lay.t),
        grid=(lay.bp, N_PAIR, nq),
        in_specs=[blk(COL_Q), seq(COL_K), seq(COL_V),
                  pl.BlockSpec((tq, LANE), lambda b, p, i: (b * nq + i, p)),
                  pl.BlockSpec((lay.t, LANE), lambda b, p, i: (b, p))],
        out_specs=pl.BlockSpec((tq, LANE), lambda b, p, i: (b * nq + i, p)),
        out_shape=jax.ShapeDtypeStruct((lay.np_rows, HW), F32),
        scratch_shapes=[pltpu.VMEM((2, lay.t, LANE), BF16), pltpu.VMEM((LANE, lay.t), BF16)],
        compiler_params=_cparams("arbitrary", "arbitrary", "arbitrary"),
        name="fox_prompt",
    )(proj, proj, proj, fq, fq)


def _fox_sample_kernel(pt_ref, qx_ref, knt_ref, vn_ref, lfnt_ref, *rest, n_pages, ts):
    kt_refs = rest[:n_pages]
    vt_refs = rest[n_pages:2 * n_pages]
    lft_refs = rest[2 * n_pages:3 * n_pages]
    o_ref = rest[3 * n_pages]
    s_ref = rest[3 * n_pages + 1]
    nr = ts * NH
    tpad = vn_ref.shape[0]
    qx = qx_ref[...].astype(BF16)
    per_q = lambda a: jnp.concatenate([a] * ts, axis=0)
    row_q = _iota((nr, 1), 0) // NH

    cum_new = per_q(_dot_exact_rhs(lfnt_ref[...], _iota((tpad, tpad), 0) <= _iota((tpad, tpad), 1), terms=3))
    t_new = _iota((nr, tpad), 1)
    fn = jnp.sum(jnp.where(t_new == row_q, cum_new, 0.0), axis=1, keepdims=True)
    s_new = jnp.dot(qx, knt_ref[...].astype(BF16), preferred_element_type=F32) + fn - cum_new
    s_new = jnp.where((t_new <= row_q) & (t_new < ts), s_new, NEG_INF)

    later = _iota((PAGE, PAGE), 0) > _iota((PAGE, PAGE), 1)
    carry = jnp.zeros((nr, 1), F32)
    m = jnp.max(s_new, axis=1, keepdims=True)
    for j in range(n_pages - 1, -1, -1):
        lft = lft_refs[j][...]
        suffix = per_q(_dot_exact_rhs(lft, later, terms=3)) + carry
        carry = carry + per_q(jnp.sum(lft, axis=1, keepdims=True))
        s = jnp.dot(qx, kt_refs[j][...].astype(BF16), preferred_element_type=F32) + fn + suffix
        s_ref[:, j * PAGE:(j + 1) * PAGE] = s
        m = jnp.maximum(m, jnp.max(s, axis=1, keepdims=True))

    p_new = jnp.exp(s_new - m)
    p_all = jnp.exp(s_ref[...] - m)
    inv = 1.0 / (jnp.sum(p_new, axis=1, keepdims=True) + jnp.sum(p_all, axis=1, keepdims=True))
    acc = _bdot(p_new * inv, vn_ref[...])
    p_all = (p_all * inv).astype(BF16)
    for j in range(n_pages):
        acc = acc + _nt(p_all[:, j * PAGE:(j + 1) * PAGE], vt_refs[j][...])
    own = _iota((NH, HW), 0) == _iota((NH, HW), 1) // DH
    rows = [jnp.sum(jnp.where(own, acc[q * NH:(q + 1) * NH, :], 0.0), axis=0, keepdims=True)
            for q in range(ts)]
    o_ref[...] = jnp.concatenate(rows, axis=0)


def _fox_sample(lay, e, page_table, qx, knt, v_new, lfnt, cache_kt, cache_vt, cache_lft):
    bs, ts = lay.bs, lay.ts
    n_pages = page_table.shape[1]
    nr = ts * NH
    tpad = v_new.shape[1]
    page = lambda j: pl.BlockSpec((None, None, HW, PAGE), lambda b, pt, j=j: (e, pt[b * n_pages + j], 0, 0))
    lfpage = lambda j: pl.BlockSpec((None, None, NH, PAGE), lambda b, pt, j=j: (e, pt[b * n_pages + j], 0, 0))
    in_specs = ([pl.BlockSpec((None, nr, HW), lambda b, pt: (b, 0, 0)),
                 pl.BlockSpec((None, HW, tpad), lambda b, pt: (b, 0, 0)),
                 pl.BlockSpec((None, tpad, HW), lambda b, pt: (b, 0, 0)),
                 pl.BlockSpec((None, NH, tpad), lambda b, pt: (b, 0, 0))]
                + [page(j) for j in range(n_pages)] + [page(j) for j in range(n_pages)]
                + [lfpage(j) for j in range(n_pages)])
    grid_spec = pltpu.PrefetchScalarGridSpec(
        num_scalar_prefetch=1, grid=(bs,), in_specs=in_specs,
        out_specs=pl.BlockSpec((None, ts, HW), lambda b, pt: (b, 0, 0)),
        scratch_shapes=[pltpu.VMEM((nr, n_pages * PAGE), F32)])
    return pl.pallas_call(
        functools.partial(_fox_sample_kernel, n_pages=n_pages, ts=ts),
        grid_spec=grid_spec,
        out_shape=jax.ShapeDtypeStruct((bs, ts, HW), F32),
        compiler_params=_cparams("arbitrary"),
        name="fox_sample",
    )(page_table.reshape(-1), qx, knt, v_new, lfnt,
      *([cache_kt] * n_pages), *([cache_vt] * n_pages), *([cache_lft] * n_pages))


def _rwkv_pre_kernel(p_ref, st_ref, mu_ref, w0_ref, w2_ref, a0_ref, a2_ref, g2_ref, kk_ref, ka_ref, rk_ref,
                     o_ref, carry_ref, *, lay, tm):
    i = pl.program_id(0)
    _, n_pt, tpb = _tile_rows(lay, tm)
    p = p_ref[...]

    @pl.when(i == 0)
    def _():
        carry_ref[...] = jnp.zeros_like(carry_ref)

    first = (i % tpb) == 0
    prev_row = jnp.where(first, 0.0, carry_ref[...])
    rolled = pltpu.roll(p, 1, 0)
    sh_prompt = jnp.where(_iota((tm, 1), 0) == 0, prev_row, rolled)
    sh_sample = jnp.concatenate([st_ref[...], p[:tm - lay.bs, :]], axis=0)
    shifted = jnp.where(i < n_pt, sh_prompt, sh_sample)
    carry_ref[...] = p[tm - 1:tm, :]
    xm = p + (shifted - p) * mu_ref[...]
    r = xm[:, 0:HW]
    k = xm[:, HW:2 * HW]
    v = xm[:, 2 * HW:3 * HW]
    la = xm[:, 3 * HW:3 * HW + LANE]
    gl = xm[:, 3 * HW + LANE:RW_IN]
    w = -_softplus(-(w0_ref[...] + _bdot(jnp.tanh(la), w2_ref[...]))) - 0.5
    a = _sigmoid(a0_ref[...] + _bdot(la, a2_ref[...]))
    g = _bdot(_sigmoid(gl), g2_ref[...])
    seg = _seg_ones(HW, DH)
    kkr = k * kk_ref[...]
    kk = kkr * lax.rsqrt(jnp.maximum(_dot_exact_rhs(kkr * kkr, seg), 1e-24))
    kmod = k * (1.0 + (a - 1.0) * ka_ref[...])
    bonus = _dot_exact_rhs(r * kmod * rk_ref[...], seg) * v
    for n, val in enumerate((r, -jnp.exp(w), kmod, v, kk, kk * a, g, bonus)):
        o_ref[:, n * HW:(n + 1) * HW] = val


def _rwkv_pre(lay, proj, state_shift, rw):
    tm = min(512, lay.tm)
    nt, _, _ = _tile_rows(lay, tm)
    row = lambda n: pl.BlockSpec((1, n), lambda i: (0, 0))
    full = lambda a, b: pl.BlockSpec((a, b), lambda i: (0, 0))
    return pl.pallas_call(
        functools.partial(_rwkv_pre_kernel, lay=lay, tm=tm),
        grid=(nt,),
        in_specs=[pl.BlockSpec((tm, RW_IN), lambda i: (i, 0)),
                  full(lay.bs, RW_IN), row(RW_IN), row(HW), full(LANE, HW), row(HW), full(LANE, HW),
                  full(LORA_G, HW), row(HW), row(HW), row(HW)],
        out_specs=pl.BlockSpec((tm, PACK_W), lambda i: (i, 0)),
        out_shape=jax.ShapeDtypeStruct((lay.mp, PACK_W), F32),
        scratch_shapes=[pltpu.VMEM((1, RW_IN), F32)],
        compiler_params=_cparams("arbitrary"),
        name="rwkv_pre",
    )(proj, state_shift, rw["mu"], rw["w0"], rw["w2"], rw["a0"], rw["a2"], rw["g2"], rw["k_k"], rw["k_a"],
      rw["r_k"])


def _rwkv_scan_kernel(*refs, c, nbb):
    x_refs = refs[:nbb]
    s0_ref, lnw_ref, lnb_ref, o_ref, so_ref, s_ref = refs[nbb:]
    ci = pl.program_id(1)

    @pl.when(ci == 0)
    def _():
        s_ref[...] = s0_ref[...]

    lane_hi = _iota((1, LANE), 1) >= DH
    tri_incl = _iota((c, c), 0) >= _iota((c, c), 1)
    rr = _iota((2 * c, 2 * c), 0)
    cc = _iota((2 * c, 2 * c), 1)
    strict = (rr % c) > (cc % c)
    incl = (rr % c) >= (cc % c)
    eye = (rr == cc).astype(F32)
    seg = _seg_ones(LANE, DH)

    def stack(x):
        return jnp.concatenate([jnp.where(lane_hi, 0.0, x), jnp.where(lane_hi, x, 0.0)], axis=0)

    chains = [(k, p) for k in range(nbb) for p in range(N_PAIR)]
    each = lambda f, *lists: [f(*args) for args in zip(*lists)]
    col = lambda n: [x_refs[k][:, n * HW + p * LANE:n * HW + (p + 1) * LANE] for k, p in chains]
    r, lw, km, v, kk, b, g, bonus = (col(n) for n in range(8))
    sbd = [s_ref[k, p] for k, p in chains]
    cum = each(lambda x: _dot_exact_lhs(tri_incl, x), lw)
    pin = each(jnp.exp, cum)
    pinv = each(lambda x: jnp.exp(-x), cum)
    a_s = each(lambda kk_, cm, lw_: stack(-kk_ * jnp.exp(cm - lw_)), kk, cum, lw)
    b_s = each(lambda x, q: stack(x * q), b, pinv)
    k_s = each(lambda x, q: stack(x * q), km, pinv)
    r_s = each(lambda x, q: stack(x * q), r, pin)
    v_s = each(stack, v)
    n_ab = each(lambda x, y: jnp.where(strict, _nt(x, y), 0.0), a_s, b_s)
    n_ak = each(lambda x, y: jnp.where(strict, _nt(x, y), 0.0), a_s, k_s)
    n_rb = each(lambda x, y: jnp.where(incl, _nt(x, y), 0.0), r_s, b_s)
    n_rk = each(lambda x, y: jnp.where(incl, _nt(x, y), 0.0), r_s, k_s)
    z = each(lambda a_, s_, n_, v_: _nt(a_, s_) + _bdot(n_, v_), a_s, sbd, n_ak, v_s)
    ys0 = each(lambda r_, s_, n_, v_: _nt(r_, s_) + _bdot(n_, v_), r_s, sbd, n_rk, v_s)
    inv = each(lambda n_: eye + n_, n_ab)
    pw = n_ab
    for _ in range(max(0, math.ceil(math.log2(c)) - 1)):
        pw = each(lambda x: _bdot(x, x), pw)
        inv = each(lambda x, q: x + _bdot(x, q), inv, pw)
    u = each(_bdot, inv, z)
    ys = each(lambda y0, n_, u_: y0 + _bdot(n_, u_), ys0, n_rb, u)
    s_new = each(lambda s_, u_, b_, v_, k_, p_: (s_ + _tn(u_, b_) + _tn(v_, k_)) * p_[c - 1:c, :],
                 sbd, u, b_s, v_s, k_s, pin)
    y = each(lambda x: x[:c] + x[c:], ys)
    mean = each(lambda x: _dot_exact_rhs(x, seg) * (1.0 / DH), y)
    dlt = each(lambda x, m_: x - m_, y, mean)
    var = each(lambda x: _dot_exact_rhs(x * x, seg) * (1.0 / DH), dlt)
    for i, (k, p) in enumerate(chains):
        s_ref[k, p] = s_new[i]
        ln = slice(p * LANE, (p + 1) * LANE)
        yn = dlt[i] * lax.rsqrt(var[i] + GN_EPS) * lnw_ref[:, ln] + lnb_ref[:, ln]
        o_ref[k, :, ln] = (yn + bonus[i]) * g[i]

    @pl.when(ci == pl.num_programs(1) - 1)
    def _():
        so_ref[...] = s_ref[...]


def _rwkv_scan(packed, s0_bd, ln_w, ln_b, nb, tn, c, nbb):
    nc = tn // c
    assert nb % nbb == 0
    xspec = lambda k: pl.BlockSpec((c, PACK_W), lambda gb, ci, k=k: ((gb * nbb + k) * nc + ci, 0))
    return pl.pallas_call(
        functools.partial(_rwkv_scan_kernel, c=c, nbb=nbb),
        grid=(nb // nbb, nc),
        in_specs=[xspec(k) for k in range(nbb)] + [
            pl.BlockSpec((nbb, N_PAIR, LANE, LANE), lambda gb, ci: (gb, 0, 0, 0)),
            pl.BlockSpec((1, HW), lambda gb, ci: (0, 0)),
            pl.BlockSpec((1, HW), lambda gb, ci: (0, 0))],
        out_specs=[pl.BlockSpec((nbb, c, HW), lambda gb, ci: (gb, ci, 0)),
                   pl.BlockSpec((nbb, N_PAIR, LANE, LANE), lambda gb, ci: (gb, 0, 0, 0))],
        out_shape=[jax.ShapeDtypeStruct((nb, tn, HW), F32),
                   jax.ShapeDtypeStruct((nb, N_PAIR, LANE, LANE), F32)],
        scratch_shapes=[pltpu.VMEM((nbb, N_PAIR, LANE, LANE), F32)],
        compiler_params=_cparams("arbitrary", "arbitrary"),
        name="rwkv_scan",
    )(*([packed] * nbb), s0_bd, ln_w, ln_b)


def _pair_states(s):
    b = s.shape[0]
    s = s.reshape(b, N_PAIR, 2, DH, DH)
    z = jnp.zeros_like(s[:, :, 0])
    top = jnp.concatenate([s[:, :, 0], z], axis=-1)
    bot = jnp.concatenate([z, s[:, :, 1]], axis=-1)
    return jnp.concatenate([top, bot], axis=-2)


def _unpair_states(sbd):
    b = sbd.shape[0]
    return jnp.stack([sbd[:, :, :DH, :DH], sbd[:, :, DH:, DH:]], axis=2).reshape(b, NH, DH, DH)


def _outproj_kernel(x_ref, ofp_ref, orp_ref, ofs_ref, ors_ref, gt_ref, w_ref, o_ref, *, lay, tm):
    i = pl.program_id(0)
    _, n_pt, _ = _tile_rows(lay, tm)
    (gate,) = _mod_params(lay, tm, i, (gt_ref,))
    is_prompt = i < n_pt
    o_fox = jnp.where(is_prompt, ofp_ref[...], ofs_ref[...])
    o_rwkv = jnp.where(is_prompt, orp_ref[...], ors_ref[...])
    y = (jnp.dot(o_fox.astype(BF16), w_ref[0:HW, :], preferred_element_type=F32)
         + jnp.dot(o_rwkv.astype(BF16), w_ref[HW:2 * HW, :], preferred_element_type=F32))
    o_ref[...] = x_ref[...] + gate * y


def _outproj(lay, x_all, o_fox_p, o_rw_p, o_fox_s, o_rw_s, mod, li, w_out):
    tm = lay.tm
    nt, n_pt, _ = _tile_rows(lay, tm)
    d = D_MODEL
    prow = lambda i: (jnp.minimum(i, n_pt - 1), 0)
    return pl.pallas_call(
        functools.partial(_outproj_kernel, lay=lay, tm=tm),
        grid=(nt,),
        in_specs=[pl.BlockSpec((tm, d), lambda i: (i, 0)),
                  pl.BlockSpec((tm, HW), prow),
                  pl.BlockSpec((tm, HW), prow),
                  pl.BlockSpec((tm, HW), lambda i: (0, 0)),
                  pl.BlockSpec((tm, HW), lambda i: (0, 0)),
                  _mod_spec(lay, li, 2, 1),
                  pl.BlockSpec((2 * HW, d), lambda i: (0, 0))],
        out_specs=pl.BlockSpec((tm, d), lambda i: (i, 0)),
        out_shape=jax.ShapeDtypeStruct((lay.mp, d), F32),
        compiler_params=_cparams("arbitrary"),
        name="outproj",
    )(x_all, o_fox_p, o_rw_p, o_fox_s, o_rw_s, mod, w_out)


def _ffn_kernel(x_ref, g_ref, sh_ref, sc_ref, gt_ref, wg_ref, wu_ref, wd_ref, o_ref, h_ref, acc_ref, *, lay, tm):
    i = pl.program_id(0)
    f = pl.program_id(1)

    @pl.when(f == 0)
    def _():
        sh, sc = _mod_params(lay, tm, i, (sh_ref, sc_ref))
        h_ref[...] = (_rms(x_ref[...], g_ref[...]) * (1.0 + sc) + sh).astype(BF16)
        acc_ref[...] = jnp.zeros_like(acc_ref)

    h = h_ref[...]
    a = jnp.dot(h, wg_ref[...].astype(BF16), preferred_element_type=F32)
    u = jnp.dot(h, wu_ref[...].astype(BF16), preferred_element_type=F32)
    hid = (a * _sigmoid(a) * u).astype(BF16)
    acc_ref[...] += jnp.dot(hid, wd_ref[...].astype(BF16), preferred_element_type=F32)

    @pl.when(f == pl.num_programs(1) - 1)
    def _():
        (gate,) = _mod_params(lay, tm, i, (gt_ref,))
        o_ref[...] = x_ref[...] + gate * acc_ref[...]


def _ffn(lay, x_all, mod, li, g, e, wg, wu, wd):
    tm = lay.tm
    nt, _, _ = _tile_rows(lay, tm)
    d = D_MODEL
    dff = wg.shape[2]
    tf = 256
    return pl.pallas_call(
        functools.partial(_ffn_kernel, lay=lay, tm=tm),
        grid=(nt, dff // tf),
        in_specs=[pl.BlockSpec((tm, d), lambda i, f: (i, 0)),
                  pl.BlockSpec((1, d), lambda i, f: (0, 0)),
                  _mod_spec(lay, li, 3, 2), _mod_spec(lay, li, 4, 2), _mod_spec(lay, li, 5, 2),
                  pl.BlockSpec((None, d, tf), lambda i, f: (e, 0, f)),
                  pl.BlockSpec((None, d, tf), lambda i, f: (e, 0, f)),
                  pl.BlockSpec((None, tf, d), lambda i, f: (e, f, 0))],
        out_specs=pl.BlockSpec((tm, d), lambda i, f: (i, 0)),
        out_shape=jax.ShapeDtypeStruct((lay.mp, d), F32),
        scratch_shapes=[pltpu.VMEM((tm, d), BF16), pltpu.VMEM((tm, d), F32)],
        compiler_params=_cparams("arbitrary", "arbitrary"),
        name="ffn",
    )(x_all, g, mod, mod, mod, wg, wu, wd)


def _pool_mix(pooled_minus_h, w_ref, ps_ref):
    pc = D_MODEL // len(POOL_WINDOWS)
    ys = [jnp.dot(pooled_minus_h[gi].astype(BF16), w_ref[gi], preferred_element_type=F32)
          for gi in range(len(POOL_WINDOWS))]
    return jnp.concatenate(ys, axis=-1) * ps_ref[...]


def _pool_prompt_kernel(x_ref, g_ref, sh_ref, sc_ref, gt_ref, w_ref, ps_ref, o_ref, hl_ref, carry_ref, *, lay, tm):
    i = pl.program_id(0)
    _, _, tpb = _tile_rows(lay, tm)
    sh, sc, gate = _mod_params(lay, tm, i, (sh_ref, sc_ref, gt_ref))
    h = _rms(x_ref[...], g_ref[...]) * (1.0 + sc) + sh

    @pl.when(i == 0)
    def _():
        carry_ref[...] = jnp.zeros_like(carry_ref)

    first = (i % tpb) == 0
    halo = jnp.where(first, 0.0, carry_ref[...])
    carry_ref[...] = h[tm - HALO:, :]
    hl_ref[...] = h[tm - HALO:, :]
    ext = jnp.concatenate([halo, h], axis=0)
    pos = (i % tpb) * tm + _iota((tm, 1), 0)
    pc = D_MODEL // len(POOL_WINDOWS)
    zs = []
    for gi, win in enumerate(POOL_WINDOWS):
        s = ext[:, gi * pc:(gi + 1) * pc]
        step = 1
        while step < win:
            s = s + pltpu.roll(s, step, 0)
            step *= 2
        cnt = jnp.minimum(pos + 1, win).astype(F32)
        zs.append(s[HALO:, :] / cnt - h[:, gi * pc:(gi + 1) * pc])
    o_ref[...] = x_ref[...] + gate * _pool_mix(zs, w_ref, ps_ref)


def _pool_prompt(lay, x_all, mod, li, g, w_pool, pool_scale):
    tm = lay.tm
    _, n_pt, tpb = _tile_rows(lay, tm)
    d = D_MODEL
    pc = d // len(POOL_WINDOWS)
    return pl.pallas_call(
        functools.partial(_pool_prompt_kernel, lay=lay, tm=tm),
        grid=(n_pt,),
        in_specs=[pl.BlockSpec((tm, d), lambda i: (i, 0)),
                  pl.BlockSpec((1, d), lambda i: (0, 0)),
                  _mod_spec(lay, li, 0, 1), _mod_spec(lay, li, 1, 1), _mod_spec(lay, li, 2, 1),
                  pl.BlockSpec((len(POOL_WINDOWS), pc, pc), lambda i: (0, 0, 0)),
                  pl.BlockSpec((1, d), lambda i: (0, 0))],
        out_specs=[pl.BlockSpec((tm, d), lambda i: (i, 0)),
                   pl.BlockSpec((None, HALO, d), lambda i: (i // tpb, 0, 0))],
        out_shape=[jax.ShapeDtypeStruct((lay.mp, d), F32),
                   jax.ShapeDtypeStruct((lay.bp, HALO, d), F32)],
        scratch_shapes=[pltpu.VMEM((HALO, d), F32)],
        input_output_aliases={0: 0},
        compiler_params=_cparams("arbitrary"),
        name="pool_prompt",
    )(x_all, g, mod, mod, mod, w_pool, pool_scale)


def _pool_sample_kernel(x_ref, buf_ref, g_ref, sh_ref, sc_ref, gt_ref, w_ref, ps_ref, o_ref, h_ref, *, lay):
    bs, ts = lay.bs, lay.ts
    sl = slice(lay.bp_pad, lay.bp_pad + bs)
    sh, sc, gate = sh_ref[sl, :], sc_ref[sl, :], gt_ref[sl, :]
    pc = D_MODEL // len(POOL_WINDOWS)
    hs = []
    for t in range(ts):
        hs.append(_rms(x_ref[t * bs:(t + 1) * bs, :], g_ref[...]) * (1.0 + sc) + sh)
        h_ref[t * bs:(t + 1) * bs, :] = hs[t]
    ext = [buf_ref[e] for e in range(POOL_BUF)] + hs
    for t in range(ts):
        zs = []
        for gi, win in enumerate(POOL_WINDOWS):
            cs = slice(gi * pc, (gi + 1) * pc)
            s = ext[POOL_BUF + t][:, cs]
            for j in range(1, win):
                s = s + ext[POOL_BUF + t - j][:, cs]
            zs.append(s * (1.0 / win) - hs[t][:, cs])
        o_ref[t * bs:(t + 1) * bs, :] = x_ref[t * bs:(t + 1) * bs, :] + gate * _pool_mix(zs, w_ref, ps_ref)


def _pool_sample(lay, x_all, buf_t, mod, li, g, w_pool, pool_scale):
    d = D_MODEL
    pc = d // len(POOL_WINDOWS)
    sblk = lay.np_rows // lay.ms
    mspec = lambda k: pl.BlockSpec((None, lay.mc, d), lambda i: (li, 0, k))
    return pl.pallas_call(
        functools.partial(_pool_sample_kernel, lay=lay),
        grid=(1,),
        in_specs=[pl.BlockSpec((lay.ms, d), lambda i: (sblk, 0)),
                  pl.BlockSpec((POOL_BUF, lay.bs, d), lambda i: (0, 0, 0)),
                  pl.BlockSpec((1, d), lambda i: (0, 0)),
                  mspec(0), mspec(1), mspec(2),
                  pl.BlockSpec((len(POOL_WINDOWS), pc, pc), lambda i: (0, 0, 0)),
                  pl.BlockSpec((1, d), lambda i: (0, 0))],
        out_specs=[pl.BlockSpec((lay.ms, d), lambda i: (sblk, 0)),
                   pl.BlockSpec((lay.ms, d), lambda i: (0, 0))],
        out_shape=[jax.ShapeDtypeStruct((lay.mp, d), F32),
                   jax.ShapeDtypeStruct((lay.ms, d), F32)],
        input_output_aliases={0: 0},
        compiler_params=_cparams("arbitrary"),
        name="pool_sample",
    )(x_all, buf_t, g, mod, mod, mod, w_pool, pool_scale)


def _router_kernel(x_ref, g_ref, sh_ref, sc_ref, rw_ref, rb_ref, h_ref, meta_ref, *, lay, tm):
    i = pl.program_id(0)
    sh, sc = _mod_params(lay, tm, i, (sh_ref, sc_ref))
    h = _rms(x_ref[...], g_ref[...]) * (1.0 + sc) + sh
    h_ref[...] = h
    lane = _iota((tm, LANE), 1)
    logits = jnp.where(lane < N_EXPERTS, _dot3(h, rw_ref[...]) + rb_ref[...], NEG_INF)
    m1 = jnp.max(logits, axis=-1, keepdims=True)
    i1 = jnp.min(jnp.where(logits == m1, lane, LANE), axis=-1, keepdims=True)
    rest = jnp.where(lane == i1, NEG_INF, logits)
    m2 = jnp.max(rest, axis=-1, keepdims=True)
    i2 = jnp.min(jnp.where(rest == m2, lane, LANE), axis=-1, keepdims=True)
    e2 = jnp.exp(m2 - m1)
    g1 = 1.0 / (1.0 + e2)
    g2 = e2 / (1.0 + e2)
    meta = jnp.where(lane == 0, i1.astype(F32), 0.0)
    meta = jnp.where(lane == 1, i2.astype(F32), meta)
    meta = jnp.where(lane == 2, g1, meta)
    meta_ref[...] = jnp.where(lane == 3, g2, meta)


def _router(lay, x_all, mod, li, g, rw_pad, rb_pad):
    tm = min(512, lay.tm)
    nt, _, _ = _tile_rows(lay, tm)
    d = D_MODEL
    return pl.pallas_call(
        functools.partial(_router_kernel, lay=lay, tm=tm),
        grid=(nt,),
        in_specs=[pl.BlockSpec((tm, d), lambda i: (i, 0)),
                  pl.BlockSpec((1, d), lambda i: (0, 0)),
                  _mod_spec(lay, li, 3, 1), _mod_spec(lay, li, 4, 1),
                  pl.BlockSpec((d, LANE), lambda i: (0, 0)),
                  pl.BlockSpec((1, LANE), lambda i: (0, 0))],
        out_specs=[pl.BlockSpec((tm, d), lambda i: (i, 0)),
                   pl.BlockSpec((tm, LANE), lambda i: (i, 0))],
        out_shape=[jax.ShapeDtypeStruct((lay.mp, d), F32),
                   jax.ShapeDtypeStruct((lay.mp, LANE), F32)],
        compiler_params=_cparams("arbitrary"),
        name="router",
    )(x_all, g, mod, mod, rw_pad, rb_pad)


def _gather_kernel(idx_ref, nv_ref, src_ref, o_ref, sem, *, rows):
    base = pl.program_id(0) * rows

    def row_copy(r):
        return pltpu.make_async_copy(src_ref.at[pl.ds(idx_ref[base + r], 1)], o_ref.at[pl.ds(r, 1)], sem)

    def start(r, c):
        row_copy(r).start()
        return c

    def wait(r, c):
        row_copy(r).wait()
        return c

    @pl.when(base < nv_ref[0])
    def _():
        lax.fori_loop(0, rows, start, 0, unroll=8)
        lax.fori_loop(0, rows, wait, 0, unroll=8)

    @pl.when(base >= nv_ref[0])
    def _():
        o_ref[...] = jnp.zeros_like(o_ref)


def _gather_rows(src, idx, n_valid, rows):
    n = idx.shape[0]
    assert n % rows == 0
    d = src.shape[1]
    grid_spec = pltpu.PrefetchScalarGridSpec(
        num_scalar_prefetch=2, grid=(n // rows,),
        in_specs=[pl.BlockSpec(memory_space=pl.ANY)],
        out_specs=pl.BlockSpec((rows, d), lambda i, idx, nv: (i, 0)),
        scratch_shapes=[pltpu.SemaphoreType.DMA(())])
    return pl.pallas_call(
        functools.partial(_gather_kernel, rows=rows), grid_spec=grid_spec,
        out_shape=jax.ShapeDtypeStruct((n, d), src.dtype),
        compiler_params=pltpu.CompilerParams(dimension_semantics=("arbitrary",), vmem_limit_bytes=VMEM_LIMIT,
                                             disable_bounds_checks=True),
        name="gather_rows",
    )(idx, n_valid, src)


def _expert_kernel(te_ref, nu_ref, x_ref, wg_ref, wu_ref, wd_ref, o_ref, h_ref, acc_ref):
    i = pl.program_id(0)
    f = pl.program_id(1)
    used = i < nu_ref[0]

    @pl.when(jnp.logical_and(used, f == 0))
    def _():
        h_ref[...] = x_ref[...].astype(BF16)
        acc_ref[...] = jnp.zeros_like(acc_ref)

    @pl.when(used)
    def _():
        h = h_ref[...]
        a = jnp.dot(h, wg_ref[...].astype(BF16), preferred_element_type=F32)
        u = jnp.dot(h, wu_ref[...].astype(BF16), preferred_element_type=F32)
        hid = (a * _sigmoid(a) * u).astype(BF16)
        acc_ref[...] += jnp.dot(hid, wd_ref[...].astype(BF16), preferred_element_type=F32)

    @pl.when(f == pl.num_programs(1) - 1)
    def _():
        o_ref[...] = jnp.where(used, acc_ref[...], 0.0)


def _experts(xg, tile_expert, n_used, o, wg, wu, wd, tme):
    mg, d = xg.shape
    dff = wg.shape[3]
    tf = 512
    nf = dff // tf

    def widx(i, f, te, nu):
        return jnp.where(i < nu[0], f, nf - 1)

    grid_spec = pltpu.PrefetchScalarGridSpec(
        num_scalar_prefetch=2, grid=(mg // tme, nf),
        in_specs=[pl.BlockSpec((tme, d), lambda i, f, te, nu: (jnp.minimum(i, nu[0] - 1), 0)),
                  pl.BlockSpec((None, None, d, tf), lambda i, f, te, nu: (o, te[i], 0, widx(i, f, te, nu))),
                  pl.BlockSpec((None, None, d, tf), lambda i, f, te, nu: (o, te[i], 0, widx(i, f, te, nu))),
                  pl.BlockSpec((None, None, tf, d), lambda i, f, te, nu: (o, te[i], widx(i, f, te, nu), 0))],
        out_specs=pl.BlockSpec((tme, d), lambda i, f, te, nu: (i, 0)),
        scratch_shapes=[pltpu.VMEM((tme, d), BF16), pltpu.VMEM((tme, d), F32)])
    return pl.pallas_call(
        _expert_kernel, grid_spec=grid_spec,
        out_shape=jax.ShapeDtypeStruct((mg, d), F32),
        compiler_params=_cparams("arbitrary", "arbitrary"),
        name="experts",
    )(tile_expert, n_used, xg, wg, wu, wd)


def _combine_kernel(d1_ref, d2_ref, x_ref, meta_ref, gt_ref, yg_ref, o_ref, y1_ref, y2_ref, sem, *, lay, tm):
    i = pl.program_id(0)
    base = i * tm

    def row_copies(r):
        return (pltpu.make_async_copy(yg_ref.at[pl.ds(d1_ref[base + r], 1)], y1_ref.at[pl.ds(r, 1)], sem.at[0]),
                pltpu.make_async_copy(yg_ref.at[pl.ds(d2_ref[base + r], 1)], y2_ref.at[pl.ds(r, 1)], sem.at[1]))

    def start(r, c):
        for cp in row_copies(r):
            cp.start()
        return c

    def wait(r, c):
        for cp in row_copies(r):
            cp.wait()
        return c

    lax.fori_loop(0, tm, start, 0, unroll=8)
    (gate,) = _mod_params(lay, tm, i, (gt_ref,))
    meta = meta_ref[...]
    lax.fori_loop(0, tm, wait, 0, unroll=8)
    y = meta[:, 2:3] * y1_ref[...] + meta[:, 3:4] * y2_ref[...]
    o_ref[...] = x_ref[...] + gate * y


def _combine(lay, x_all, yg, dest2, meta, mod, li):
    tm = lay.tm
    nt, _, _ = _tile_rows(lay, tm)
    d = D_MODEL
    grid_spec = pltpu.PrefetchScalarGridSpec(
        num_scalar_prefetch=2, grid=(nt,),
        in_specs=[pl.BlockSpec((tm, d), lambda i, d1, d2: (i, 0)),
                  pl.BlockSpec((tm, LANE), lambda i, d1, d2: (i, 0)),
                  pl.BlockSpec((None, lay.mc, d), lambda i, d1, d2: (li, 0, 5)),
                  pl.BlockSpec(memory_space=pl.ANY)],
        out_specs=pl.BlockSpec((tm, d), lambda i, d1, d2: (i, 0)),
        scratch_shapes=[pltpu.VMEM((tm, d), F32), pltpu.VMEM((tm, d), F32), pltpu.SemaphoreType.DMA((2,))])
    return pl.pallas_call(
        functools.partial(_combine_kernel, lay=lay, tm=tm),
        grid_spec=grid_spec,
        out_shape=jax.ShapeDtypeStruct((lay.mp, d), F32),
        compiler_params=pltpu.CompilerParams(dimension_semantics=("arbitrary",), vmem_limit_bytes=VMEM_LIMIT,
                                             disable_bounds_checks=True),
        name="combine",
    )(dest2[:, 0], dest2[:, 1], x_all, meta, mod, yg)


def _moe(lay, x_all, mod, li, g, router_w, router_b, o, wg, wu, wd):
    d = D_MODEL
    rw_pad = jnp.pad(router_w, ((0, 0), (0, LANE - N_EXPERTS)))
    rb_pad = jnp.pad(router_b, (0, LANE - N_EXPERTS)).reshape(1, LANE)
    h2, meta = _router(lay, x_all, mod, li, g, rw_pad, rb_pad)

    ntok = lay.ntok
    tme = lay.tm
    ex = meta[:ntok, 0:2].astype(jnp.int32).reshape(-1)
    onehot = (ex[:, None] == jnp.arange(N_EXPERTS)[None, :]).astype(jnp.int32)
    rank = jnp.sum((jnp.cumsum(onehot, axis=0) - onehot) * onehot, axis=1)
    counts = jnp.sum(onehot, axis=0)
    tiles = (counts + tme - 1) // tme
    tile_end = jnp.cumsum(tiles)
    offs = (tile_end - tiles) * tme
    dest = offs[ex] + rank
    n_tiles = (2 * ntok + N_EXPERTS * (tme - 1)) // tme + 1
    mg = n_tiles * tme
    src = jnp.zeros((mg,), jnp.int32).at[dest].set(jnp.arange(2 * ntok, dtype=jnp.int32) // 2)
    tile_expert = jnp.minimum(jnp.sum(jnp.arange(n_tiles)[:, None] >= tile_end[None, :], axis=1),
                              N_EXPERTS - 1).astype(jnp.int32)
    n_used = tile_end[-1:].astype(jnp.int32)
    last_e = tile_expert[jnp.maximum(n_used[0] - 1, 0)]
    tile_expert = jnp.where(jnp.arange(n_tiles) < n_used[0], tile_expert, last_e)

    xg = _gather_rows(h2, src, n_used * tme, tme)
    yg = _experts(xg, tile_expert, n_used, o, wg, wu, wd, tme)
    dest2 = jnp.pad(dest.reshape(ntok, 2), ((0, lay.mp - ntok), (0, 0)))
    return _combine(lay, x_all, yg, dest2, meta, mod, li)


def _perm_rwkv_cols(a):
    r, wl, k, v, al, gl = jnp.split(a, [HW, HW + LORA_W, 2 * HW + LORA_W, 3 * HW + LORA_W, 3 * HW + LORA_W + LORA_A],
                                    axis=-1)
    return jnp.concatenate([r, k, v, wl, al, gl], axis=-1)


def _unperm_rwkv_cols(a):
    r, k, v, wl, al, gl = jnp.split(a, [HW, 2 * HW, 3 * HW, 3 * HW + LORA_W, 3 * HW + LORA_W + LORA_A], axis=-1)
    return jnp.concatenate([r, wl, k, v, al, gl], axis=-1)


def kernel(x_prompt, x_sample, c_prompt, c_sample, cache_fox_k, cache_fox_v, cache_fox_logf, page_table, state_rwkv, state_rwkv_shift, state_pool, norm1_g, norm2_g, ada_w, ada_b, w_in, w_out, fox_q_gain, fox_k_gain, fox_f_bias, rwkv_mu, rwkv_w0, rwkv_w2, rwkv_a0, rwkv_a2, rwkv_g2, rwkv_k_k, rwkv_k_a, rwkv_r_k, rwkv_ln_w, rwkv_ln_b, ffn_w_gate, ffn_w_up, ffn_w_down, pool_w, pool_scale, moe_router_w, moe_router_b, moe_w_gate, moe_w_up, moe_w_down):
    bp, t, d = x_prompt.shape
    bs, ts, _ = x_sample.shape
    assert d == D_MODEL
    lay = _Layout(bp, t, bs, ts)
    depth = ada_w.shape[0]
    n_phys = cache_fox_k.shape[1]
    fox_in = 3 * HW + NH

    xs_tb = jnp.swapaxes(x_sample, 0, 1).reshape(lay.ms, d)
    x_all = jnp.concatenate([x_prompt.reshape(lay.np_rows, d), xs_tb, jnp.zeros((lay.mp - lay.ntok, d), F32)], axis=0)
    c_all = jnp.concatenate([c_prompt, jnp.zeros((lay.bp_pad - bp, d), F32), c_sample], axis=0)
    mod = _ada(c_all, ada_w, ada_b)

    slot_minor = lambda c: jnp.transpose(c, (0, 1, 3, 4, 2)).reshape(c.shape[0], n_phys, HW, PAGE)
    cache_kt = slot_minor(cache_fox_k)
    cache_vt = slot_minor(cache_fox_v)
    cache_lft = jnp.swapaxes(cache_fox_logf, 2, 3)
    head_of_lane = jnp.arange(HW) // DH
    qmask = (jnp.arange(NH)[:, None] == head_of_lane[None, :]).astype(F32)
    tpad = 8

    outs = {k: [] for k in ("kp", "ks", "vp", "vs", "fp", "fs", "sp", "ss", "shp", "shs", "pp", "ps")}
    for li in range(depth):
        g1 = norm1_g[li].reshape(1, d)
        g2 = norm2_g[li].reshape(1, d)
        if li % 2 == 0:
            e = li // 2
            wi = w_in[e]
            w_cat = jnp.concatenate(
                [_perm_rwkv_cols(wi[:, fox_in:]), wi[:, :3 * HW],
                 jnp.pad(wi[:, 3 * HW:fox_in], ((0, 0), (0, F_PAD - NH)))], axis=1).astype(BF16)
            qk_gain = jnp.concatenate([jnp.tile(fox_q_gain[e], NH), jnp.tile(fox_k_gain[e], NH)]).reshape(1, 2 * HW)
            f_bias = jnp.pad(fox_f_bias[e], (0, F_PAD - NH)).reshape(1, F_PAD)
            proj = _inproj(lay, x_all, mod, li, g1, w_cat, qk_gain, f_bias)

            o_fox_p = _fox_prompt(lay, proj, _cumf(lay, proj))

            ps_rows = proj[lay.np_rows:lay.ntok].reshape(ts, bs, PW).swapaxes(0, 1)
            q_s = ps_rows[..., COL_Q:COL_K] * (DH ** -0.5)
            k_s = ps_rows[..., COL_K:COL_V]
            v_s = ps_rows[..., COL_V:COL_F]
            lf_s = ps_rows[..., COL_F:COL_F + NH]
            qx = (q_s[:, :, None, :] * qmask[None, None, :, :]).reshape(bs, ts * NH, HW)
            padt = lambda a: jnp.pad(a, ((0, 0), (0, tpad - ts), (0, 0)))
            o_fox_s = _fox_sample(lay, e, page_table, qx, jnp.swapaxes(padt(k_s), 1, 2), padt(v_s),
                                  jnp.swapaxes(padt(lf_s), 1, 2), cache_kt, cache_vt, cache_lft)
            sample_tile = lambda a: jnp.pad(a, ((0, lay.tm - lay.ms), (0, 0)))
            o_fox_s = sample_tile(jnp.swapaxes(o_fox_s, 0, 1).reshape(lay.ms, HW))

            pad_lora = lambda w_, top: jnp.pad(w_, ((0, LANE - w_.shape[0]), (0, 0)) if top else
                                               ((LANE - w_.shape[0], 0), (0, 0))).astype(BF16)
            rw = dict(mu=_perm_rwkv_cols(rwkv_mu[e]).reshape(1, RW_IN), w0=rwkv_w0[e].reshape(1, HW),
                      w2=pad_lora(rwkv_w2[e], True), a0=rwkv_a0[e].reshape(1, HW), a2=pad_lora(rwkv_a2[e], False),
                      g2=rwkv_g2[e].astype(BF16), k_k=rwkv_k_k[e].reshape(1, HW), k_a=rwkv_k_a[e].reshape(1, HW),
                      r_k=rwkv_r_k[e].reshape(1, HW))
            packed = _rwkv_pre(lay, proj, _perm_rwkv_cols(state_rwkv_shift[e]), rw)
            ln_w = rwkv_ln_w[e].reshape(1, HW)
            ln_b = rwkv_ln_b[e].reshape(1, HW)
            cp = min(64, t)
            o_rw_p, s_p = _rwkv_scan(packed, jnp.zeros((bp, N_PAIR, LANE, LANE), F32), ln_w, ln_b,
                                     bp, t, cp, 2 if bp % 2 == 0 else 1)
            pk_s = packed[lay.np_rows:lay.ntok].reshape(ts, bs, PACK_W).swapaxes(0, 1)
            pk_s = jnp.pad(pk_s, ((0, 0), (0, tpad - ts), (0, 0))).reshape(bs * tpad, PACK_W)
            o_rw_s, s_s = _rwkv_scan(pk_s, _pair_states(state_rwkv[e]), ln_w, ln_b, bs, tpad, tpad,
                                     4 if bs % 4 == 0 else 1)
            o_rw_s = sample_tile(o_rw_s[:, :ts].swapaxes(0, 1).reshape(lay.ms, HW))

            x_all = _outproj(lay, x_all, o_fox_p, o_rw_p.reshape(lay.np_rows, HW), o_fox_s, o_rw_s, mod, li,
                             w_out[e].astype(BF16))
            x_all = _ffn(lay, x_all, mod, li, g2, e, ffn_w_gate, ffn_w_up, ffn_w_down)

            outs["kp"].append(proj[:lay.np_rows, COL_K:COL_V].reshape(bp, t, NH, DH))
            outs["vp"].append(proj[:lay.np_rows, COL_V:COL_F].reshape(bp, t, NH, DH))
            outs["fp"].append(proj[:lay.np_rows, COL_F:COL_F + NH].reshape(bp, t, NH))
            outs["ks"].append(k_s.reshape(bs, ts, NH, DH))
            outs["vs"].append(v_s.reshape(bs, ts, NH, DH))
            outs["fs"].append(lf_s)
            outs["sp"].append(_unpair_states(s_p))
            outs["ss"].append(_unpair_states(s_s))
            last_p = proj[t - 1:lay.np_rows:t, :RW_IN]
            last_s = proj[lay.np_rows + (ts - 1) * bs:lay.ntok, :RW_IN]
            outs["shp"].append(_unperm_rwkv_cols(last_p))
            outs["shs"].append(_unperm_rwkv_cols(last_s))
        else:
            o = li // 2
            w_pool = pool_w[o].astype(BF16)
            p_scale = pool_scale[o].reshape(1, d)
            x_all, hl = _pool_prompt(lay, x_all, mod, li, g1, w_pool, p_scale)
            buf_t = jnp.swapaxes(state_pool[o], 0, 1)
            x_all, h_s = _pool_sample(lay, x_all, buf_t, mod, li, g1, w_pool, p_scale)
            outs["pp"].append(hl[:, HALO - POOL_BUF:, :])
            h_s_bt = jnp.swapaxes(h_s.reshape(ts, bs, d), 0, 1)
            outs["ps"].append(jnp.concatenate([state_pool[o], h_s_bt], axis=1)[:, -POOL_BUF:])
            x_all = _moe(lay, x_all, mod, li, g2, moe_router_w[o], moe_router_b[o],
                         o, moe_w_gate, moe_w_up, moe_w_down)

    y_p = x_all[:lay.np_rows].reshape(bp, t, d)
    y_s = jnp.swapaxes(x_all[lay.np_rows:lay.ntok].reshape(ts, bs, d), 0, 1)
    st = lambda k: jnp.stack(outs[k])
    return (y_p, y_s, st("kp"), st("ks"), st("vp"), st("vs"), st("fp"), st("fs"), st("sp"), st("ss"),
            st("shp"), st("shs"), st("pp"), st("ps"))
```

```python
import functools
import math

import jax
import jax.numpy as jnp
from jax import lax
from jax.experimental import pallas as pl
from jax.experimental.pallas import tpu as pltpu

F32 = jnp.float32
BF16 = jnp.bfloat16

NH = 8
DH = 64
HW = NH * DH
N_PAIR = NH // 2
LANE = 128
LORA_W = 64
LORA_A = 64
LORA_G = 128
RW_IN = 3 * HW + LORA_W + LORA_A + LORA_G
F_PAD = 256
PW = RW_IN + 3 * HW + F_PAD
COL_Q = RW_IN
COL_K = RW_IN + HW
COL_V = RW_IN + 2 * HW
COL_F = RW_IN + 3 * HW
PACK_W = 8 * HW
POOL_WINDOWS = (2, 4, 8, 16)
POOL_BUF = 15
HALO = 16
N_EXPERTS = 8
RMS_EPS = 1e-6
GN_EPS = 64e-5
NEG_INF = -1e30
PAGE = 128
VMEM_LIMIT = 56 * 1024 * 1024


def _cparams(*sem):
    return pltpu.CompilerParams(dimension_semantics=sem, vmem_limit_bytes=VMEM_LIMIT)


def _bdot(a, b):
    return jnp.dot(a.astype(BF16), b.astype(BF16), preferred_element_type=F32)


def _nt(a, b):
    return lax.dot_general(a.astype(BF16), b.astype(BF16), (((1,), (1,)), ((), ())),
                           preferred_element_type=F32)


def _tn(a, b):
    return lax.dot_general(a.astype(BF16), b.astype(BF16), (((0,), (0,)), ((), ())),
                           preferred_element_type=F32)


def _split(x, terms):
    out = []
    for _ in range(terms - 1):
        h = x.astype(BF16)
        out.append(h)
        x = x - h.astype(F32)
    out.append(x.astype(BF16))
    return out


def _dot_exact_lhs(a01, x, terms=3):
    a = a01.astype(BF16)
    acc = None
    for t in _split(x, terms):
        d = jnp.dot(a, t, preferred_element_type=F32)
        acc = d if acc is None else acc + d
    return acc


def _dot_exact_rhs(x, b01, terms=2):
    b = b01.astype(BF16)
    acc = None
    for t in _split(x, terms):
        d = jnp.dot(t, b, preferred_element_type=F32)
        acc = d if acc is None else acc + d
    return acc


def _dot3(a, b):
    ah, al = _split(a, 2)
    bh, bl = _split(b, 2)
    return (jnp.dot(ah, bh, preferred_element_type=F32) + jnp.dot(ah, bl, preferred_element_type=F32)
            + jnp.dot(al, bh, preferred_element_type=F32))


def _sigmoid(x):
    return 1.0 / (1.0 + jnp.exp(-x))


def _softplus(x):
    return jnp.maximum(x, 0.0) + jnp.log(1.0 + jnp.exp(-jnp.abs(x)))


def _iota(shape, dim):
    return lax.broadcasted_iota(jnp.int32, shape, dim)


def _seg_ones(n, seg):
    return (_iota((n, n), 0) // seg == _iota((n, n), 1) // seg).astype(BF16)


class _Layout:
    def __init__(self, bp, t, bs, ts):
        self.bp, self.t, self.bs, self.ts = bp, t, bs, ts
        self.np_rows = bp * t
        self.ms = bs * ts
        self.tm = min(1024, t)
        assert t % self.tm == 0 and self.tm % bs == 0 and self.ms <= self.tm
        assert self.np_rows % self.ms == 0 and bs % 8 == 0
        self.mp = self.np_rows + self.tm
        self.ntok = self.np_rows + self.ms
        self.bp_pad = 8 * ((bp + 7) // 8)
        self.mc = self.bp_pad + bs


def _tile_rows(lay, tm):
    assert lay.tm % tm == 0 and tm % lay.bs == 0
    return lay.mp // tm, lay.np_rows // tm, lay.t // tm


def _mod_params(lay, tm, i, refs):
    _, n_pt, tpb = _tile_rows(lay, tm)
    is_prompt = i < n_pt
    b = jnp.minimum(i // tpb, lay.bp - 1)
    out = []
    for r in refs:
        p_row = r[pl.ds(b, 1), :]
        s_blk = r[lay.bp_pad:lay.bp_pad + lay.bs, :]
        s_rows = jnp.concatenate([s_blk] * (tm // lay.bs), axis=0)
        out.append(jnp.where(is_prompt, p_row, s_rows))
    return out


def _rms(x, g):
    return x * lax.rsqrt(jnp.mean(x * x, axis=-1, keepdims=True) + RMS_EPS) * g


def _mod_spec(lay, li, k, nidx):
    if nidx == 1:
        return pl.BlockSpec((None, lay.mc, D_MODEL), lambda i: (li, 0, k))
    return pl.BlockSpec((None, lay.mc, D_MODEL), lambda i, j: (li, 0, k))


D_MODEL = 1024


def _ada_kernel(c_ref, w_ref, b_ref, o_ref):
    c = c_ref[...]
    s = c * _sigmoid(c)
    o_ref[...] = _dot3(s, w_ref[...]) + b_ref[...]


def _ada(c_all, ada_w, ada_b):
    nl, d, n = ada_w.shape
    mc = c_all.shape[0]
    tn = 512
    return pl.pallas_call(
        _ada_kernel,
        grid=(nl, n // tn),
        in_specs=[pl.BlockSpec((mc, d), lambda l, j: (0, 0)),
                  pl.BlockSpec((None, d, tn), lambda l, j: (l, 0, j)),
                  pl.BlockSpec((None, 1, tn), lambda l, j: (l, 0, j))],
        out_specs=pl.BlockSpec((None, mc, tn), lambda l, j: (l, 0, j)),
        out_shape=jax.ShapeDtypeStruct((nl, mc, n), F32),
        compiler_params=_cparams("arbitrary", "arbitrary"),
        name="ada",
    )(c_all, ada_w, ada_b.reshape(nl, 1, n))


def _inproj_kernel(x_ref, g_ref, sh_ref, sc_ref, w_ref, qkg_ref, fb_ref, o_ref, *, lay, tm):
    i = pl.program_id(0)
    sh, sc = _mod_params(lay, tm, i, (sh_ref, sc_ref))
    h = (_rms(x_ref[...], g_ref[...]) * (1.0 + sc) + sh).astype(BF16)
    ch = 256
    seg = _seg_ones(ch, DH)
    for c in range(PW // ch):
        c0 = c * ch
        acc = jnp.dot(h, w_ref[:, c0:c0 + ch], preferred_element_type=F32)
        if COL_Q <= c0 < COL_V:
            ss = _dot_exact_rhs(acc * acc, seg)
            acc = acc * lax.rsqrt(ss * (1.0 / DH) + RMS_EPS) * qkg_ref[:, c0 - COL_Q:c0 - COL_Q + ch]
        elif c0 >= COL_F:
            z = acc + fb_ref[...]
            acc = -_softplus(-z)
        o_ref[:, c0:c0 + ch] = acc


def _inproj(lay, x_all, mod, li, g, w_cat, qk_gain, f_bias):
    tm = min(512, lay.tm)
    nt, _, _ = _tile_rows(lay, tm)
    d = D_MODEL
    return pl.pallas_call(
        functools.partial(_inproj_kernel, lay=lay, tm=tm),
        grid=(nt,),
        in_specs=[pl.BlockSpec((tm, d), lambda i: (i, 0)),
                  pl.BlockSpec((1, d), lambda i: (0, 0)),
                  _mod_spec(lay, li, 0, 1), _mod_spec(lay, li, 1, 1),
                  pl.BlockSpec((d, PW), lambda i: (0, 0)),
                  pl.BlockSpec((1, 2 * HW), lambda i: (0, 0)),
                  pl.BlockSpec((1, F_PAD), lambda i: (0, 0))],
        out_specs=pl.BlockSpec((tm, PW), lambda i: (i, 0)),
        out_shape=jax.ShapeDtypeStruct((lay.mp, PW), F32),
        compiler_params=_cparams("arbitrary"),
        name="inproj",
    )(x_all, g, mod, mod, w_cat, qk_gain, f_bias)


def _cumf_kernel(lf_ref, fq_ref, carry_ref, *, tc):
    c = pl.program_id(1)

    @pl.when(c == 0)
    def _():
        carry_ref[...] = jnp.zeros_like(carry_ref)

    tri = _iota((tc, tc), 0) >= _iota((tc, tc), 1)
    f = _dot_exact_lhs(tri, lf_ref[...]) + carry_ref[...]
    carry_ref[...] = f[tc - 1:tc, :]
    expand = _iota((LANE, HW), 0) == _iota((LANE, HW), 1) // DH
    fq_ref[...] = _dot_exact_rhs(f, expand, terms=3)


def _cumf(lay, proj):
    tc = min(256, lay.t)
    npc = lay.t // tc
    return pl.pallas_call(
        functools.partial(_cumf_kernel, tc=tc),
        grid=(lay.bp, npc),
        in_specs=[pl.BlockSpec((tc, LANE), lambda b, c: (b * npc + c, COL_F // LANE))],
        out_specs=pl.BlockSpec((tc, HW), lambda b, c: (b * npc + c, 0)),
        out_shape=jax.ShapeDtypeStruct((lay.np_rows, HW), F32),
        scratch_shapes=[pltpu.VMEM((1, LANE), F32)],
        compiler_params=_cparams("arbitrary", "arbitrary"),
        name="cumf",
    )(proj)


N_BIAS = 3


def _bias_lanes(x, spare0, terms, ones_first):
    lane = _iota((1, LANE), 1)
    t_off, o_off = (N_BIAS, 0) if ones_first else (0, N_BIAS)
    for n, tval in enumerate(terms):
        x = jnp.where(lane == spare0 + t_off + n, tval, x)
    return jnp.where((lane >= spare0 + o_off) & (lane < spare0 + o_off + N_BIAS), 1.0, x)


def _split_f32(x):
    out = []
    for _ in range(N_BIAS - 1):
        h = x.astype(BF16).astype(F32)
        out.append(h)
        x = x - h
    out.append(x)
    return out


def _fox_prompt_kernel(q_ref, k_ref, v_ref, fq_ref, fs_ref, o_ref, kb_ref, vt_ref, *, tq, t):
    qi = pl.program_id(2)
    log2e = 1.4426950408889634
    lane_hi = _iota((1, LANE), 1) >= DH
    spare = (DH, 0)

    @pl.when(qi == 0)
    def _():
        for c0 in range(0, t, tq):
            kc = k_ref[c0:c0 + tq, :]
            fs = fs_ref[c0:c0 + tq, :] * log2e
            for hh in range(2):
                neg = [-x for x in _split_f32(fs[:, hh * DH:hh * DH + 1])]
                kb_ref[hh, c0:c0 + tq, :] = _bias_lanes(kc, spare[hh], neg, ones_first=False).astype(BF16)
            vt_ref[:, c0:c0 + tq] = v_ref[c0:c0 + tq, :].T.astype(BF16)

    q = q_ref[...] * (DH ** -0.5 * log2e)
    fq = fq_ref[...] * log2e
    qh = []
    for hh in range(2):
        qm = jnp.where(lane_hi, q, 0.0) if hh else jnp.where(lane_hi, 0.0, q)
        qh.append(_bias_lanes(qm, spare[hh], _split_f32(fq[:, hh * DH:hh * DH + 1]), ones_first=True).astype(BF16))
    key_le_query = _iota((tq, tq), 0) <= _iota((tq, tq), 1)

    def step(kj, carry, diagonal):
        k0 = pl.multiple_of(kj * tq, tq)
        out = []
        for hh in range(2):
            m, l, acc = carry[hh]
            st = lax.dot_general(kb_ref[hh, pl.ds(k0, tq), :], qh[hh], (((1,), (1,)), ((), ())),
                                 preferred_element_type=F32)
            if diagonal:
                st = jnp.where(key_le_query, st, NEG_INF)
            m_new = jnp.maximum(m, jnp.max(st, axis=0, keepdims=True))
            p = jnp.exp2(st - m_new)
            alpha = jnp.exp2(m - m_new)
            l = alpha * l + jnp.sum(p, axis=0, keepdims=True)
            acc = alpha * acc + jnp.dot(vt_ref[hh * DH:(hh + 1) * DH, pl.ds(k0, tq)], p.astype(BF16),
                                        preferred_element_type=F32)
            out.append((m_new, l, acc))
        return tuple(out)

    init = (jnp.full((1, tq), NEG_INF, F32), jnp.zeros((1, tq), F32), jnp.zeros((DH, tq), F32))
    carry = lax.fori_loop(0, qi, lambda kj, c: step(kj, c, False), (init, init))
    (_, l0, acc0), (_, l1, acc1) = step(qi, carry, True)
    o_ref[...] = jnp.concatenate([acc0 / l0, acc1 / l1], axis=0).T


def _fox_prompt(lay, proj, fq):
    tq = min(512, lay.t)
    nq = lay.t // tq
    blk = lambda col: pl.BlockSpec((tq, LANE), lambda b, p, i: (b * nq + i, col // LANE + p))
    seq = lambda col: pl.BlockSpec((lay.t, LANE), lambda b, p, i: (b, col // LANE + p))
    return pl.pallas_call(
        functools.partial(_fox_prompt_kernel, tq=tq, t=lay.t),
        grid=(lay.bp, N_PAIR, nq),
        in_specs=[blk(COL_Q), seq(COL_K), seq(COL_V),
                  pl.BlockSpec((tq, LANE), lambda b, p, i: (b * nq + i, p)),
                  pl.BlockSpec((lay.t, LANE), lambda b, p, i: (b, p))],
        out_specs=pl.BlockSpec((tq, LANE), lambda b, p, i: (b * nq + i, p)),
        out_shape=jax.ShapeDtypeStruct((lay.np_rows, HW), F32),
        scratch_shapes=[pltpu.VMEM((2, lay.t, LANE), BF16), pltpu.VMEM((LANE, lay.t), BF16)],
        compiler_params=_cparams("arbitrary", "arbitrary", "arbitrary"),
        name="fox_prompt",
    )(proj, proj, proj, fq, fq)


def _fox_sample_kernel(pt_ref, qx_ref, knt_ref, vn_ref, lfnt_ref, *rest, n_pages, ts):
    kt_refs = rest[:n_pages]
    vt_refs = rest[n_pages:2 * n_pages]
    lft_refs = rest[2 * n_pages:3 * n_pages]
    o_ref = rest[3 * n_pages]
    s_ref = rest[3 * n_pages + 1]
    nr = ts * NH
    tpad = vn_ref.shape[0]
    qx = qx_ref[...].astype(BF16)
    per_q = lambda a: jnp.concatenate([a] * ts, axis=0)
    row_q = _iota((nr, 1), 0) // NH

    cum_new = per_q(_dot_exact_rhs(lfnt_ref[...], _iota((tpad, tpad), 0) <= _iota((tpad, tpad), 1), terms=3))
    t_new = _iota((nr, tpad), 1)
    fn = jnp.sum(jnp.where(t_new == row_q, cum_new, 0.0), axis=1, keepdims=True)
    s_new = jnp.dot(qx, knt_ref[...].astype(BF16), preferred_element_type=F32) + fn - cum_new
    s_new = jnp.where((t_new <= row_q) & (t_new < ts), s_new, NEG_INF)

    later = _iota((PAGE, PAGE), 0) > _iota((PAGE, PAGE), 1)
    carry = jnp.zeros((nr, 1), F32)
    m = jnp.max(s_new, axis=1, keepdims=True)
    for j in range(n_pages - 1, -1, -1):
        lft = lft_refs[j][...]
        suffix = per_q(_dot_exact_rhs(lft, later, terms=3)) + carry
        carry = carry + per_q(jnp.sum(lft, axis=1, keepdims=True))
        s = jnp.dot(qx, kt_refs[j][...].astype(BF16), preferred_element_type=F32) + fn + suffix
        s_ref[:, j * PAGE:(j + 1) * PAGE] = s
        m = jnp.maximum(m, jnp.max(s, axis=1, keepdims=True))

    p_new = jnp.exp(s_new - m)
    p_all = jnp.exp(s_ref[...] - m)
    inv = 1.0 / (jnp.sum(p_new, axis=1, keepdims=True) + jnp.sum(p_all, axis=1, keepdims=True))
    acc = _bdot(p_new * inv, vn_ref[...])
    p_all = (p_all * inv).astype(BF16)
    for j in range(n_pages):
        acc = acc + _nt(p_all[:, j * PAGE:(j + 1) * PAGE], vt_refs[j][...])
    own = _iota((NH, HW), 0) == _iota((NH, HW), 1) // DH
    rows = [jnp.sum(jnp.where(own, acc[q * NH:(q + 1) * NH, :], 0.0), axis=0, keepdims=True)
            for q in range(ts)]
    o_ref[...] = jnp.concatenate(rows, axis=0)


def _fox_sample(lay, e, page_table, qx, knt, v_new, lfnt, cache_kt, cache_vt, cache_lft):
    bs, ts = lay.bs, lay.ts
    n_pages = page_table.shape[1]
    nr = ts * NH
    tpad = v_new.shape[1]
    page = lambda j: pl.BlockSpec((None, None, HW, PAGE), lambda b, pt, j=j: (e, pt[b * n_pages + j], 0, 0))
    lfpage = lambda j: pl.BlockSpec((None, None, NH, PAGE), lambda b, pt, j=j: (e, pt[b * n_pages + j], 0, 0))
    in_specs = ([pl.BlockSpec((None, nr, HW), lambda b, pt: (b, 0, 0)),
                 pl.BlockSpec((None, HW, tpad), lambda b, pt: (b, 0, 0)),
                 pl.BlockSpec((None, tpad, HW), lambda b, pt: (b, 0, 0)),
                 pl.BlockSpec((None, NH, tpad), lambda b, pt: (b, 0, 0))]
                + [page(j) for j in range(n_pages)] + [page(j) for j in range(n_pages)]
                + [lfpage(j) for j in range(n_pages)])
    grid_spec = pltpu.PrefetchScalarGridSpec(
        num_scalar_prefetch=1, grid=(bs,), in_specs=in_specs,
        out_specs=pl.BlockSpec((None, ts, HW), lambda b, pt: (b, 0, 0)),
        scratch_shapes=[pltpu.VMEM((nr, n_pages * PAGE), F32)])
    return pl.pallas_call(
        functools.partial(_fox_sample_kernel, n_pages=n_pages, ts=ts),
        grid_spec=grid_spec,
        out_shape=jax.ShapeDtypeStruct((bs, ts, HW), F32),
        compiler_params=_cparams("arbitrary"),
        name="fox_sample",
    )(page_table.reshape(-1), qx, knt, v_new, lfnt,
      *([cache_kt] * n_pages), *([cache_vt] * n_pages), *([cache_lft] * n_pages))


def _rwkv_pre_kernel(p_ref, st_ref, mu_ref, w0_ref, w2_ref, a0_ref, a2_ref, g2_ref, kk_ref, ka_ref, rk_ref,
                     o_ref, carry_ref, *, lay, tm):
    i = pl.program_id(0)
    _, n_pt, tpb = _tile_rows(lay, tm)
    p = p_ref[...]

    @pl.when(i == 0)
    def _():
        carry_ref[...] = jnp.zeros_like(carry_ref)

    first = (i % tpb) == 0
    prev_row = jnp.where(first, 0.0, carry_ref[...])
    rolled = pltpu.roll(p, 1, 0)
    sh_prompt = jnp.where(_iota((tm, 1), 0) == 0, prev_row, rolled)
    sh_sample = jnp.concatenate([st_ref[...], p[:tm - lay.bs, :]], axis=0)
    shifted = jnp.where(i < n_pt, sh_prompt, sh_sample)
    carry_ref[...] = p[tm - 1:tm, :]
    xm = p + (shifted - p) * mu_ref[...]
    r = xm[:, 0:HW]
    k = xm[:, HW:2 * HW]
    v = xm[:, 2 * HW:3 * HW]
    la = xm[:, 3 * HW:3 * HW + LANE]
    gl = xm[:, 3 * HW + LANE:RW_IN]
    w = -_softplus(-(w0_ref[...] + _bdot(jnp.tanh(la), w2_ref[...]))) - 0.5
    a = _sigmoid(a0_ref[...] + _bdot(la, a2_ref[...]))
    g = _bdot(_sigmoid(gl), g2_ref[...])
    seg = _seg_ones(HW, DH)
    kkr = k * kk_ref[...]
    kk = kkr * lax.rsqrt(jnp.maximum(_dot_exact_rhs(kkr * kkr, seg), 1e-24))
    kmod = k * (1.0 + (a - 1.0) * ka_ref[...])
    bonus = _dot_exact_rhs(r * kmod * rk_ref[...], seg) * v
    for n, val in enumerate((r, -jnp.exp(w), kmod, v, kk, kk * a, g, bonus)):
        o_ref[:, n * HW:(n + 1) * HW] = val


def _rwkv_pre(lay, proj, state_shift, rw):
    tm = min(512, lay.tm)
    nt, _, _ = _tile_rows(lay, tm)
    row = lambda n: pl.BlockSpec((1, n), lambda i: (0, 0))
    full = lambda a, b: pl.BlockSpec((a, b), lambda i: (0, 0))
    return pl.pallas_call(
        functools.partial(_rwkv_pre_kernel, lay=lay, tm=tm),
        grid=(nt,),
        in_specs=[pl.BlockSpec((tm, RW_IN), lambda i: (i, 0)),
                  full(lay.bs, RW_IN), row(RW_IN), row(HW), full(LANE, HW), row(HW), full(LANE, HW),
                  full(LORA_G, HW), row(HW), row(HW), row(HW)],
        out_specs=pl.BlockSpec((tm, PACK_W), lambda i: (i, 0)),
        out_shape=jax.ShapeDtypeStruct((lay.mp, PACK_W), F32),
        scratch_shapes=[pltpu.VMEM((1, RW_IN), F32)],
        compiler_params=_cparams("arbitrary"),
        name="rwkv_pre",
    )(proj, state_shift, rw["mu"], rw["w0"], rw["w2"], rw["a0"], rw["a2"], rw["g2"], rw["k_k"], rw["k_a"],
      rw["r_k"])


def _rwkv_scan_kernel(*refs, c, nbb):
    x_refs = refs[:nbb]
    s0_ref, lnw_ref, lnb_ref, o_ref, so_ref, s_ref = refs[nbb:]
    ci = pl.program_id(1)

    @pl.when(ci == 0)
    def _():
        s_ref[...] = s0_ref[...]

    lane_hi = _iota((1, LANE), 1) >= DH
    tri_incl = _iota((c, c), 0) >= _iota((c, c), 1)
    rr = _iota((2 * c, 2 * c), 0)
    cc = _iota((2 * c, 2 * c), 1)
    strict = (rr % c) > (cc % c)
    incl = (rr % c) >= (cc % c)
    eye = (rr == cc).astype(F32)
    seg = _seg_ones(LANE, DH)

    def stack(x):
        return jnp.concatenate([jnp.where(lane_hi, 0.0, x), jnp.where(lane_hi, x, 0.0)], axis=0)

    chains = [(k, p) for k in range(nbb) for p in range(N_PAIR)]
    each = lambda f, *lists: [f(*args) for args in zip(*lists)]
    col = lambda n: [x_refs[k][:, n * HW + p * LANE:n * HW + (p + 1) * LANE] for k, p in chains]
    r, lw, km, v, kk, b, g, bonus = (col(n) for n in range(8))
    sbd = [s_ref[k, p] for k, p in chains]
    cum = each(lambda x: _dot_exact_lhs(tri_incl, x), lw)
    pin = each(jnp.exp, cum)
    pinv = each(lambda x: jnp.exp(-x), cum)
    a_s = each(lambda kk_, cm, lw_: stack(-kk_ * jnp.exp(cm - lw_)), kk, cum, lw)
    b_s = each(lambda x, q: stack(x * q), b, pinv)
    k_s = each(lambda x, q: stack(x * q), km, pinv)
    r_s = each(lambda x, q: stack(x * q), r, pin)
    v_s = each(stack, v)
    n_ab = each(lambda x, y: jnp.where(strict, _nt(x, y), 0.0), a_s, b_s)
    n_ak = each(lambda x, y: jnp.where(strict, _nt(x, y), 0.0), a_s, k_s)
    n_rb = each(lambda x, y: jnp.where(incl, _nt(x, y), 0.0), r_s, b_s)
    n_rk = each(lambda x, y: jnp.where(incl, _nt(x, y), 0.0), r_s, k_s)
    z = each(lambda a_, s_, n_, v_: _nt(a_, s_) + _bdot(n_, v_), a_s, sbd, n_ak, v_s)
    ys0 = each(lambda r_, s_, n_, v_: _nt(r_, s_) + _bdot(n_, v_), r_s, sbd, n_rk, v_s)
    inv = each(lambda n_: eye + n_, n_ab)
    pw = n_ab
    for _ in range(max(0, math.ceil(math.log2(c)) - 1)):
        pw = each(lambda x: _bdot(x, x), pw)
        inv = each(lambda x, q: x + _bdot(x, q), inv, pw)
    u = each(_bdot, inv, z)
    ys = each(lambda y0, n_, u_: y0 + _bdot(n_, u_), ys0, n_rb, u)
    s_new = each(lambda s_, u_, b_, v_, k_, p_: (s_ + _tn(u_, b_) + _tn(v_, k_)) * p_[c - 1:c, :],
                 sbd, u, b_s, v_s, k_s, pin)
    y = each(lambda x: x[:c] + x[c:], ys)
    mean = each(lambda x: _dot_exact_rhs(x, seg, terms=1) * (1.0 / DH), y)
    dlt = each(lambda x, m_: x - m_, y, mean)
    var = each(lambda x: _dot_exact_rhs(x * x, seg, terms=1) * (1.0 / DH), dlt)
    for i, (k, p) in enumerate(chains):
        s_ref[k, p] = s_new[i]
        ln = slice(p * LANE, (p + 1) * LANE)
        yn = dlt[i] * lax.rsqrt(var[i] + GN_EPS) * lnw_ref[:, ln] + lnb_ref[:, ln]
        o_ref[k, :, ln] = (yn + bonus[i]) * g[i]

    @pl.when(ci == pl.num_programs(1) - 1)
    def _():
        so_ref[...] = s_ref[...]


def _rwkv_scan(packed, s0_bd, ln_w, ln_b, nb, tn, c, nbb):
    nc = tn // c
    assert nb % nbb == 0
    xspec = lambda k: pl.BlockSpec((c, PACK_W), lambda gb, ci, k=k: ((gb * nbb + k) * nc + ci, 0))
    return pl.pallas_call(
        functools.partial(_rwkv_scan_kernel, c=c, nbb=nbb),
        grid=(nb // nbb, nc),
        in_specs=[xspec(k) for k in range(nbb)] + [
            pl.BlockSpec((nbb, N_PAIR, LANE, LANE), lambda gb, ci: (gb, 0, 0, 0)),
            pl.BlockSpec((1, HW), lambda gb, ci: (0, 0)),
            pl.BlockSpec((1, HW), lambda gb, ci: (0, 0))],
        out_specs=[pl.BlockSpec((nbb, c, HW), lambda gb, ci: (gb, ci, 0)),
                   pl.BlockSpec((nbb, N_PAIR, LANE, LANE), lambda gb, ci: (gb, 0, 0, 0))],
        out_shape=[jax.ShapeDtypeStruct((nb, tn, HW), F32),
                   jax.ShapeDtypeStruct((nb, N_PAIR, LANE, LANE), F32)],
        scratch_shapes=[pltpu.VMEM((nbb, N_PAIR, LANE, LANE), F32)],
        compiler_params=_cparams("arbitrary", "arbitrary"),
        name="rwkv_scan",
    )(*([packed] * nbb), s0_bd, ln_w, ln_b)


def _pair_states(s):
    b = s.shape[0]
    s = s.reshape(b, N_PAIR, 2, DH, DH)
    z = jnp.zeros_like(s[:, :, 0])
    top = jnp.concatenate([s[:, :, 0], z], axis=-1)
    bot = jnp.concatenate([z, s[:, :, 1]], axis=-1)
    return jnp.concatenate([top, bot], axis=-2)


def _unpair_states(sbd):
    b = sbd.shape[0]
    return jnp.stack([sbd[:, :, :DH, :DH], sbd[:, :, DH:, DH:]], axis=2).reshape(b, NH, DH, DH)


def _outproj_kernel(x_ref, ofp_ref, orp_ref, ofs_ref, ors_ref, gt_ref, w_ref, o_ref, *, lay, tm):
    i = pl.program_id(0)
    _, n_pt, _ = _tile_rows(lay, tm)
    (gate,) = _mod_params(lay, tm, i, (gt_ref,))
    is_prompt = i < n_pt
    o_fox = jnp.where(is_prompt, ofp_ref[...], ofs_ref[...])
    o_rwkv = jnp.where(is_prompt, orp_ref[...], ors_ref[...])
    y = (jnp.dot(o_fox.astype(BF16), w_ref[0:HW, :], preferred_element_type=F32)
         + jnp.dot(o_rwkv.astype(BF16), w_ref[HW:2 * HW, :], preferred_element_type=F32))
    o_ref[...] = x_ref[...] + gate * y


def _outproj(lay, x_all, o_fox_p, o_rw_p, o_fox_s, o_rw_s, mod, li, w_out):
    tm = lay.tm
    nt, n_pt, _ = _tile_rows(lay, tm)
    d = D_MODEL
    prow = lambda i: (jnp.minimum(i, n_pt - 1), 0)
    return pl.pallas_call(
        functools.partial(_outproj_kernel, lay=lay, tm=tm),
        grid=(nt,),
        in_specs=[pl.BlockSpec((tm, d), lambda i: (i, 0)),
                  pl.BlockSpec((tm, HW), prow),
                  pl.BlockSpec((tm, HW), prow),
                  pl.BlockSpec((tm, HW), lambda i: (0, 0)),
                  pl.BlockSpec((tm, HW), lambda i: (0, 0)),
                  _mod_spec(lay, li, 2, 1),
                  pl.BlockSpec((2 * HW, d), lambda i: (0, 0))],
        out_specs=pl.BlockSpec((tm, d), lambda i: (i, 0)),
        out_shape=jax.ShapeDtypeStruct((lay.mp, d), F32),
        compiler_params=_cparams("arbitrary"),
        name="outproj",
    )(x_all, o_fox_p, o_rw_p, o_fox_s, o_rw_s, mod, w_out)


def _ffn_kernel(x_ref, g_ref, sh_ref, sc_ref, gt_ref, wg_ref, wu_ref, wd_ref, o_ref, h_ref, acc_ref, *, lay, tm):
    i = pl.program_id(0)
    f = pl.program_id(1)

    @pl.when(f == 0)
    def _():
        sh, sc = _mod_params(lay, tm, i, (sh_ref, sc_ref))
        h_ref[...] = (_rms(x_ref[...], g_ref[...]) * (1.0 + sc) + sh).astype(BF16)
        acc_ref[...] = jnp.zeros_like(acc_ref)

    h = h_ref[...]
    a = jnp.dot(h, wg_ref[...].astype(BF16), preferred_element_type=F32)
    u = jnp.dot(h, wu_ref[...].astype(BF16), preferred_element_type=F32)
    hid = (a * _sigmoid(a) * u).astype(BF16)
    acc_ref[...] += jnp.dot(hid, wd_ref[...].astype(BF16), preferred_element_type=F32)

    @pl.when(f == pl.num_programs(1) - 1)
    def _():
        (gate,) = _mod_params(lay, tm, i, (gt_ref,))
        o_ref[...] = x_ref[...] + gate * acc_ref[...]


def _ffn(lay, x_all, mod, li, g, e, wg, wu, wd):
    tm = lay.tm
    nt, _, _ = _tile_rows(lay, tm)
    d = D_MODEL
    dff = wg.shape[2]
    tf = 256
    return pl.pallas_call(
        functools.partial(_ffn_kernel, lay=lay, tm=tm),
        grid=(nt, dff // tf),
        in_specs=[pl.BlockSpec((tm, d), lambda i, f: (i, 0)),
                  pl.BlockSpec((1, d), lambda i, f: (0, 0)),
                  _mod_spec(lay, li, 3, 2), _mod_spec(lay, li, 4, 2), _mod_spec(lay, li, 5, 2),
                  pl.BlockSpec((None, d, tf), lambda i, f: (e, 0, f)),
                  pl.BlockSpec((None, d, tf), lambda i, f: (e, 0, f)),
                  pl.BlockSpec((None, tf, d), lambda i, f: (e, f, 0))],
        out_specs=pl.BlockSpec((tm, d), lambda i, f: (i, 0)),
        out_shape=jax.ShapeDtypeStruct((lay.mp, d), F32),
        scratch_shapes=[pltpu.VMEM((tm, d), BF16), pltpu.VMEM((tm, d), F32)],
        compiler_params=_cparams("arbitrary", "arbitrary"),
        name="ffn",
    )(x_all, g, mod, mod, mod, wg, wu, wd)


def _pool_mix(pooled_minus_h, w_ref, ps_ref):
    pc = D_MODEL // len(POOL_WINDOWS)
    ys = [jnp.dot(pooled_minus_h[gi].astype(BF16), w_ref[gi], preferred_element_type=F32)
          for gi in range(len(POOL_WINDOWS))]
    return jnp.concatenate(ys, axis=-1) * ps_ref[...]


def _pool_prompt_kernel(x_ref, g_ref, sh_ref, sc_ref, gt_ref, w_ref, ps_ref, o_ref, hl_ref, carry_ref, *, lay, tm):
    i = pl.program_id(0)
    _, _, tpb = _tile_rows(lay, tm)
    sh, sc, gate = _mod_params(lay, tm, i, (sh_ref, sc_ref, gt_ref))
    h = _rms(x_ref[...], g_ref[...]) * (1.0 + sc) + sh

    @pl.when(i == 0)
    def _():
        carry_ref[...] = jnp.zeros_like(carry_ref)

    first = (i % tpb) == 0
    halo = jnp.where(first, 0.0, carry_ref[...])
    carry_ref[...] = h[tm - HALO:, :]
    hl_ref[...] = h[tm - HALO:, :]
    ext = jnp.concatenate([halo, h], axis=0)
    pos = (i % tpb) * tm + _iota((tm, 1), 0)
    pc = D_MODEL // len(POOL_WINDOWS)
    zs = []
    for gi, win in enumerate(POOL_WINDOWS):
        s = ext[:, gi * pc:(gi + 1) * pc]
        step = 1
        while step < win:
            s = s + pltpu.roll(s, step, 0)
            step *= 2
        cnt = jnp.minimum(pos + 1, win).astype(F32)
        zs.append(s[HALO:, :] / cnt - h[:, gi * pc:(gi + 1) * pc])
    o_ref[...] = x_ref[...] + gate * _pool_mix(zs, w_ref, ps_ref)


def _pool_prompt(lay, x_all, mod, li, g, w_pool, pool_scale):
    tm = lay.tm
    _, n_pt, tpb = _tile_rows(lay, tm)
    d = D_MODEL
    pc = d // len(POOL_WINDOWS)
    return pl.pallas_call(
        functools.partial(_pool_prompt_kernel, lay=lay, tm=tm),
        grid=(n_pt,),
        in_specs=[pl.BlockSpec((tm, d), lambda i: (i, 0)),
                  pl.BlockSpec((1, d), lambda i: (0, 0)),
                  _mod_spec(lay, li, 0, 1), _mod_spec(lay, li, 1, 1), _mod_spec(lay, li, 2, 1),
                  pl.BlockSpec((len(POOL_WINDOWS), pc, pc), lambda i: (0, 0, 0)),
                  pl.BlockSpec((1, d), lambda i: (0, 0))],
        out_specs=[pl.BlockSpec((tm, d), lambda i: (i, 0)),
                   pl.BlockSpec((None, HALO, d), lambda i: (i // tpb, 0, 0))],
        out_shape=[jax.ShapeDtypeStruct((lay.mp, d), F32),
                   jax.ShapeDtypeStruct((lay.bp, HALO, d), F32)],
        scratch_shapes=[pltpu.VMEM((HALO, d), F32)],
        input_output_aliases={0: 0},
        compiler_params=_cparams("arbitrary"),
        name="pool_prompt",
    )(x_all, g, mod, mod, mod, w_pool, pool_scale)


def _pool_sample_kernel(x_ref, buf_ref, g_ref, sh_ref, sc_ref, gt_ref, w_ref, ps_ref, o_ref, h_ref, *, lay):
    bs, ts = lay.bs, lay.ts
    sl = slice(lay.bp_pad, lay.bp_pad + bs)
    sh, sc, gate = sh_ref[sl, :], sc_ref[sl, :], gt_ref[sl, :]
    pc = D_MODEL // len(POOL_WINDOWS)
    hs = []
    for t in range(ts):
        hs.append(_rms(x_ref[t * bs:(t + 1) * bs, :], g_ref[...]) * (1.0 + sc) + sh)
        h_ref[t * bs:(t + 1) * bs, :] = hs[t]
    ext = [buf_ref[e] for e in range(POOL_BUF)] + hs
    for t in range(ts):
        zs = []
        for gi, win in enumerate(POOL_WINDOWS):
            cs = slice(gi * pc, (gi + 1) * pc)
            s = ext[POOL_BUF + t][:, cs]
            for j in range(1, win):
                s = s + ext[POOL_BUF + t - j][:, cs]
            zs.append(s * (1.0 / win) - hs[t][:, cs])
        o_ref[t * bs:(t + 1) * bs, :] = x_ref[t * bs:(t + 1) * bs, :] + gate * _pool_mix(zs, w_ref, ps_ref)


def _pool_sample(lay, x_all, buf_t, mod, li, g, w_pool, pool_scale):
    d = D_MODEL
    pc = d // len(POOL_WINDOWS)
    sblk = lay.np_rows // lay.ms
    mspec = lambda k: pl.BlockSpec((None, lay.mc, d), lambda i: (li, 0, k))
    return pl.pallas_call(
        functools.partial(_pool_sample_kernel, lay=lay),
        grid=(1,),
        in_specs=[pl.BlockSpec((lay.ms, d), lambda i: (sblk, 0)),
                  pl.BlockSpec((POOL_BUF, lay.bs, d), lambda i: (0, 0, 0)),
                  pl.BlockSpec((1, d), lambda i: (0, 0)),
                  mspec(0), mspec(1), mspec(2),
                  pl.BlockSpec((len(POOL_WINDOWS), pc, pc), lambda i: (0, 0, 0)),
                  pl.BlockSpec((1, d), lambda i: (0, 0))],
        out_specs=[pl.BlockSpec((lay.ms, d), lambda i: (sblk, 0)),
                   pl.BlockSpec((lay.ms, d), lambda i: (0, 0))],
        out_shape=[jax.ShapeDtypeStruct((lay.mp, d), F32),
                   jax.ShapeDtypeStruct((lay.ms, d), F32)],
        input_output_aliases={0: 0},
        compiler_params=_cparams("arbitrary"),
        name="pool_sample",
    )(x_all, buf_t, g, mod, mod, mod, w_pool, pool_scale)


def _router_kernel(x_ref, g_ref, sh_ref, sc_ref, rw_ref, rb_ref, h_ref, meta_ref, *, lay, tm):
    i = pl.program_id(0)
    sh, sc = _mod_params(lay, tm, i, (sh_ref, sc_ref))
    h = _rms(x_ref[...], g_ref[...]) * (1.0 + sc) + sh
    h_ref[...] = h
    lane = _iota((tm, LANE), 1)
    logits = jnp.where(lane < N_EXPERTS, _dot3(h, rw_ref[...]) + rb_ref[...], NEG_INF)
    m1 = jnp.max(logits, axis=-1, keepdims=True)
    i1 = jnp.min(jnp.where(logits == m1, lane, LANE), axis=-1, keepdims=True)
    rest = jnp.where(lane == i1, NEG_INF, logits)
    m2 = jnp.max(rest, axis=-1, keepdims=True)
    i2 = jnp.min(jnp.where(rest == m2, lane, LANE), axis=-1, keepdims=True)
    e2 = jnp.exp(m2 - m1)
    g1 = 1.0 / (1.0 + e2)
    g2 = e2 / (1.0 + e2)
    meta = jnp.where(lane == 0, i1.astype(F32), 0.0)
    meta = jnp.where(lane == 1, i2.astype(F32), meta)
    meta = jnp.where(lane == 2, g1, meta)
    meta_ref[...] = jnp.where(lane == 3, g2, meta)


def _router(lay, x_all, mod, li, g, rw_pad, rb_pad):
    tm = min(512, lay.tm)
    nt, _, _ = _tile_rows(lay, tm)
    d = D_MODEL
    return pl.pallas_call(
        functools.partial(_router_kernel, lay=lay, tm=tm),
        grid=(nt,),
        in_specs=[pl.BlockSpec((tm, d), lambda i: (i, 0)),
                  pl.BlockSpec((1, d), lambda i: (0, 0)),
                  _mod_spec(lay, li, 3, 1), _mod_spec(lay, li, 4, 1),
                  pl.BlockSpec((d, LANE), lambda i: (0, 0)),
                  pl.BlockSpec((1, LANE), lambda i: (0, 0))],
        out_specs=[pl.BlockSpec((tm, d), lambda i: (i, 0)),
                   pl.BlockSpec((tm, LANE), lambda i: (i, 0))],
        out_shape=[jax.ShapeDtypeStruct((lay.mp, d), F32),
                   jax.ShapeDtypeStruct((lay.mp, LANE), F32)],
        compiler_params=_cparams("arbitrary"),
        name="router",
    )(x_all, g, mod, mod, rw_pad, rb_pad)


def _dispatch_kernel(dest_ref, h_ref, z_ref, xg_ref, sem, *, tm):
    del z_ref
    base = pl.program_id(0) * tm

    def row_copies(r):
        return tuple(pltpu.make_async_copy(h_ref.at[pl.ds(r, 1)], xg_ref.at[pl.ds(dest_ref[2 * (base + r) + k], 1)],
                                           sem.at[k]) for k in range(2))

    def start(r, c):
        for cp in row_copies(r):
            cp.start()
        return c

    def wait(r, c):
        for cp in row_copies(r):
            cp.wait()
        return c

    lax.fori_loop(0, tm, start, 0, unroll=8)
    lax.fori_loop(0, tm, wait, 0, unroll=8)


def _dispatch_rows(lay, h2, dest, n_slots):
    tm = lay.tm
    d = h2.shape[1]
    grid_spec = pltpu.PrefetchScalarGridSpec(
        num_scalar_prefetch=1, grid=(lay.mp // tm,),
        in_specs=[pl.BlockSpec((tm, d), lambda i, dest: (i, 0)),
                  pl.BlockSpec(memory_space=pl.ANY)],
        out_specs=pl.BlockSpec(memory_space=pl.ANY),
        scratch_shapes=[pltpu.SemaphoreType.DMA((2,))])
    return pl.pallas_call(
        functools.partial(_dispatch_kernel, tm=tm), grid_spec=grid_spec,
        out_shape=jax.ShapeDtypeStruct((n_slots, d), h2.dtype),
        input_output_aliases={2: 0},
        compiler_params=pltpu.CompilerParams(dimension_semantics=("arbitrary",), vmem_limit_bytes=VMEM_LIMIT,
                                             disable_bounds_checks=True),
        name="dispatch_rows",
    )(dest, h2, jnp.zeros((n_slots, d), h2.dtype))


def _expert_kernel(te_ref, nu_ref, x_ref, wg_ref, wu_ref, wd_ref, o_ref, h_ref, acc_ref):
    i = pl.program_id(0)
    f = pl.program_id(1)
    used = i < nu_ref[0]

    @pl.when(jnp.logical_and(used, f == 0))
    def _():
        h_ref[...] = x_ref[...].astype(BF16)
        acc_ref[...] = jnp.zeros_like(acc_ref)

    @pl.when(used)
    def _():
        h = h_ref[...]
        a = jnp.dot(h, wg_ref[...].astype(BF16), preferred_element_type=F32)
        u = jnp.dot(h, wu_ref[...].astype(BF16), preferred_element_type=F32)
        hid = (a * _sigmoid(a) * u).astype(BF16)
        acc_ref[...] += jnp.dot(hid, wd_ref[...].astype(BF16), preferred_element_type=F32)

    @pl.when(f == pl.num_programs(1) - 1)
    def _():
        o_ref[...] = jnp.where(used, acc_ref[...], 0.0)


def _experts(xg, mg, tile_expert, n_used, o, wg, wu, wd, tme):
    d = xg.shape[1]
    dff = wg.shape[3]
    tf = 512
    nf = dff // tf

    def widx(i, f, te, nu):
        return jnp.where(i < nu[0], f, nf - 1)

    grid_spec = pltpu.PrefetchScalarGridSpec(
        num_scalar_prefetch=2, grid=(mg // tme, nf),
        in_specs=[pl.BlockSpec((tme, d), lambda i, f, te, nu: (jnp.minimum(i, nu[0] - 1), 0)),
                  pl.BlockSpec((None, None, d, tf), lambda i, f, te, nu: (o, te[i], 0, widx(i, f, te, nu))),
                  pl.BlockSpec((None, None, d, tf), lambda i, f, te, nu: (o, te[i], 0, widx(i, f, te, nu))),
                  pl.BlockSpec((None, None, tf, d), lambda i, f, te, nu: (o, te[i], widx(i, f, te, nu), 0))],
        out_specs=pl.BlockSpec((tme, d), lambda i, f, te, nu: (i, 0)),
        scratch_shapes=[pltpu.VMEM((tme, d), BF16), pltpu.VMEM((tme, d), F32)])
    return pl.pallas_call(
        _expert_kernel, grid_spec=grid_spec,
        out_shape=jax.ShapeDtypeStruct((mg, d), F32),
        compiler_params=_cparams("arbitrary", "arbitrary"),
        name="experts",
    )(tile_expert, n_used, xg, wg, wu, wd)


def _combine_kernel(d1_ref, d2_ref, x_ref, meta_ref, gt_ref, yg_ref, o_ref, y1_ref, y2_ref, sem, *, lay, tm):
    i = pl.program_id(0)
    base = i * tm

    def row_copies(r):
        return (pltpu.make_async_copy(yg_ref.at[pl.ds(d1_ref[base + r], 1)], y1_ref.at[pl.ds(r, 1)], sem.at[0]),
                pltpu.make_async_copy(yg_ref.at[pl.ds(d2_ref[base + r], 1)], y2_ref.at[pl.ds(r, 1)], sem.at[1]))

    def start(r, c):
        for cp in row_copies(r):
            cp.start()
        return c

    def wait(r, c):
        for cp in row_copies(r):
            cp.wait()
        return c

    lax.fori_loop(0, tm, start, 0, unroll=8)
    (gate,) = _mod_params(lay, tm, i, (gt_ref,))
    meta = meta_ref[...]
    lax.fori_loop(0, tm, wait, 0, unroll=8)
    y = meta[:, 2:3] * y1_ref[...] + meta[:, 3:4] * y2_ref[...]
    o_ref[...] = x_ref[...] + gate * y


def _combine(lay, x_all, yg, dest2, meta, mod, li):
    tm = lay.tm
    nt, _, _ = _tile_rows(lay, tm)
    d = D_MODEL
    grid_spec = pltpu.PrefetchScalarGridSpec(
        num_scalar_prefetch=2, grid=(nt,),
        in_specs=[pl.BlockSpec((tm, d), lambda i, d1, d2: (i, 0)),
                  pl.BlockSpec((tm, LANE), lambda i, d1, d2: (i, 0)),
                  pl.BlockSpec((None, lay.mc, d), lambda i, d1, d2: (li, 0, 5)),
                  pl.BlockSpec(memory_space=pl.ANY)],
        out_specs=pl.BlockSpec((tm, d), lambda i, d1, d2: (i, 0)),
        scratch_shapes=[pltpu.VMEM((tm, d), F32), pltpu.VMEM((tm, d), F32), pltpu.SemaphoreType.DMA((2,))])
    return pl.pallas_call(
        functools.partial(_combine_kernel, lay=lay, tm=tm),
        grid_spec=grid_spec,
        out_shape=jax.ShapeDtypeStruct((lay.mp, d), F32),
        compiler_params=pltpu.CompilerParams(dimension_semantics=("arbitrary",), vmem_limit_bytes=VMEM_LIMIT,
                                             disable_bounds_checks=True),
        name="combine",
    )(dest2[:, 0], dest2[:, 1], x_all, meta, mod, yg)


def _moe(lay, x_all, mod, li, g, router_w, router_b, o, wg, wu, wd):
    d = D_MODEL
    rw_pad = jnp.pad(router_w, ((0, 0), (0, LANE - N_EXPERTS)))
    rb_pad = jnp.pad(router_b, (0, LANE - N_EXPERTS)).reshape(1, LANE)
    h2, meta = _router(lay, x_all, mod, li, g, rw_pad, rb_pad)

    ntok = lay.ntok
    tme = lay.tm
    ex = meta[:ntok, 0:2].astype(jnp.int32).reshape(-1)
    onehot = (ex[:, None] == jnp.arange(N_EXPERTS)[None, :]).astype(jnp.int32)
    rank = jnp.sum((jnp.cumsum(onehot, axis=0) - onehot) * onehot, axis=1)
    counts = jnp.sum(onehot, axis=0)
    tiles = (counts + tme - 1) // tme
    tile_end = jnp.cumsum(tiles)
    offs = (tile_end - tiles) * tme
    dest = offs[ex] + rank
    n_tiles = (2 * ntok + N_EXPERTS * (tme - 1)) // tme + 1
    mg = n_tiles * tme
    tile_expert = jnp.minimum(jnp.sum(jnp.arange(n_tiles)[:, None] >= tile_end[None, :], axis=1),
                              N_EXPERTS - 1).astype(jnp.int32)
    n_used = tile_end[-1:].astype(jnp.int32)
    last_e = tile_expert[jnp.maximum(n_used[0] - 1, 0)]
    tile_expert = jnp.where(jnp.arange(n_tiles) < n_used[0], tile_expert, last_e)

    spare = mg + jnp.arange(2 * (lay.mp - ntok), dtype=jnp.int32)
    xg = _dispatch_rows(lay, h2, jnp.concatenate([dest.astype(jnp.int32), spare]), mg + spare.shape[0])
    yg = _experts(xg, mg, tile_expert, n_used, o, wg, wu, wd, tme)
    dest2 = jnp.pad(dest.reshape(ntok, 2), ((0, lay.mp - ntok), (0, 0)))
    return _combine(lay, x_all, yg, dest2, meta, mod, li)


def _perm_rwkv_cols(a):
    r, wl, k, v, al, gl = jnp.split(a, [HW, HW + LORA_W, 2 * HW + LORA_W, 3 * HW + LORA_W, 3 * HW + LORA_W + LORA_A],
                                    axis=-1)
    return jnp.concatenate([r, k, v, wl, al, gl], axis=-1)


def _unperm_rwkv_cols(a):
    r, k, v, wl, al, gl = jnp.split(a, [HW, 2 * HW, 3 * HW, 3 * HW + LORA_W, 3 * HW + LORA_W + LORA_A], axis=-1)
    return jnp.concatenate([r, wl, k, v, al, gl], axis=-1)


def kernel(x_prompt, x_sample, c_prompt, c_sample, cache_fox_k, cache_fox_v, cache_fox_logf, page_table, state_rwkv, state_rwkv_shift, state_pool, norm1_g, norm2_g, ada_w, ada_b, w_in, w_out, fox_q_gain, fox_k_gain, fox_f_bias, rwkv_mu, rwkv_w0, rwkv_w2, rwkv_a0, rwkv_a2, rwkv_g2, rwkv_k_k, rwkv_k_a, rwkv_r_k, rwkv_ln_w, rwkv_ln_b, ffn_w_gate, ffn_w_up, ffn_w_down, pool_w, pool_scale, moe_router_w, moe_router_b, moe_w_gate, moe_w_up, moe_w_down):
    bp, t, d = x_prompt.shape
    bs, ts, _ = x_sample.shape
    assert d == D_MODEL
    lay = _Layout(bp, t, bs, ts)
    depth = ada_w.shape[0]
    n_phys = cache_fox_k.shape[1]
    fox_in = 3 * HW + NH

    xs_tb = jnp.swapaxes(x_sample, 0, 1).reshape(lay.ms, d)
    x_all = jnp.concatenate([x_prompt.reshape(lay.np_rows, d), xs_tb, jnp.zeros((lay.mp - lay.ntok, d), F32)], axis=0)
    c_all = jnp.concatenate([c_prompt, jnp.zeros((lay.bp_pad - bp, d), F32), c_sample], axis=0)
    mod = _ada(c_all, ada_w, ada_b)

    slot_minor = lambda c: jnp.transpose(c, (0, 1, 3, 4, 2)).reshape(c.shape[0], n_phys, HW, PAGE)
    cache_kt = slot_minor(cache_fox_k)
    cache_vt = slot_minor(cache_fox_v)
    cache_lft = jnp.swapaxes(cache_fox_logf, 2, 3)
    head_of_lane = jnp.arange(HW) // DH
    qmask = (jnp.arange(NH)[:, None] == head_of_lane[None, :]).astype(F32)
    tpad = 8

    outs = {k: [] for k in ("kp", "ks", "vp", "vs", "fp", "fs", "sp", "ss", "shp", "shs", "pp", "ps")}
    for li in range(depth):
        g1 = norm1_g[li].reshape(1, d)
        g2 = norm2_g[li].reshape(1, d)
        if li % 2 == 0:
            e = li // 2
            wi = w_in[e]
            w_cat = jnp.concatenate(
                [_perm_rwkv_cols(wi[:, fox_in:]), wi[:, :3 * HW],
                 jnp.pad(wi[:, 3 * HW:fox_in], ((0, 0), (0, F_PAD - NH)))], axis=1).astype(BF16)
            qk_gain = jnp.concatenate([jnp.tile(fox_q_gain[e], NH), jnp.tile(fox_k_gain[e], NH)]).reshape(1, 2 * HW)
            f_bias = jnp.pad(fox_f_bias[e], (0, F_PAD - NH)).reshape(1, F_PAD)
            proj = _inproj(lay, x_all, mod, li, g1, w_cat, qk_gain, f_bias)

            o_fox_p = _fox_prompt(lay, proj, _cumf(lay, proj))

            ps_rows = proj[lay.np_rows:lay.ntok].reshape(ts, bs, PW).swapaxes(0, 1)
            q_s = ps_rows[..., COL_Q:COL_K] * (DH ** -0.5)
            k_s = ps_rows[..., COL_K:COL_V]
            v_s = ps_rows[..., COL_V:COL_F]
            lf_s = ps_rows[..., COL_F:COL_F + NH]
            qx = (q_s[:, :, None, :] * qmask[None, None, :, :]).reshape(bs, ts * NH, HW)
            padt = lambda a: jnp.pad(a, ((0, 0), (0, tpad - ts), (0, 0)))
            o_fox_s = _fox_sample(lay, e, page_table, qx, jnp.swapaxes(padt(k_s), 1, 2), padt(v_s),
                                  jnp.swapaxes(padt(lf_s), 1, 2), cache_kt, cache_vt, cache_lft)
            sample_tile = lambda a: jnp.pad(a, ((0, lay.tm - lay.ms), (0, 0)))
            o_fox_s = sample_tile(jnp.swapaxes(o_fox_s, 0, 1).reshape(lay.ms, HW))

            pad_lora = lambda w_, top: jnp.pad(w_, ((0, LANE - w_.shape[0]), (0, 0)) if top else
                                               ((LANE - w_.shape[0], 0), (0, 0))).astype(BF16)
            rw = dict(mu=_perm_rwkv_cols(rwkv_mu[e]).reshape(1, RW_IN), w0=rwkv_w0[e].reshape(1, HW),
                      w2=pad_lora(rwkv_w2[e], True), a0=rwkv_a0[e].reshape(1, HW), a2=pad_lora(rwkv_a2[e], False),
                      g2=rwkv_g2[e].astype(BF16), k_k=rwkv_k_k[e].reshape(1, HW), k_a=rwkv_k_a[e].reshape(1, HW),
                      r_k=rwkv_r_k[e].reshape(1, HW))
            packed = _rwkv_pre(lay, proj, _perm_rwkv_cols(state_rwkv_shift[e]), rw)
            ln_w = rwkv_ln_w[e].reshape(1, HW)
            ln_b = rwkv_ln_b[e].reshape(1, HW)
            cp = min(64, t)
            o_rw_p, s_p = _rwkv_scan(packed, jnp.zeros((bp, N_PAIR, LANE, LANE), F32), ln_w, ln_b,
                                     bp, t, cp, 2 if bp % 2 == 0 else 1)
            pk_s = packed[lay.np_rows:lay.ntok].reshape(ts, bs, PACK_W).swapaxes(0, 1)
            pk_s = jnp.pad(pk_s, ((0, 0), (0, tpad - ts), (0, 0))).reshape(bs * tpad, PACK_W)
            o_rw_s, s_s = _rwkv_scan(pk_s, _pair_states(state_rwkv[e]), ln_w, ln_b, bs, tpad, tpad,
                                     4 if bs % 4 == 0 else 1)
            o_rw_s = sample_tile(o_rw_s[:, :ts].swapaxes(0, 1).reshape(lay.ms, HW))

            x_all = _outproj(lay, x_all, o_fox_p, o_rw_p.reshape(lay.np_rows, HW), o_fox_s, o_rw_s, mod, li,
                             w_out[e].astype(BF16))
            x_all = _ffn(lay, x_all, mod, li, g2, e, ffn_w_gate, ffn_w_up, ffn_w_down)

            outs["kp"].append(proj[:lay.np_rows, COL_K:COL_V].reshape(bp, t, NH, DH))
            outs["vp"].append(proj[:lay.np_rows, COL_V:COL_F].reshape(bp, t, NH, DH))
            outs["fp"].append(proj[:lay.np_rows, COL_F:COL_F + NH].reshape(bp, t, NH))
            outs["ks"].append(k_s.reshape(bs, ts, NH, DH))
            outs["vs"].append(v_s.reshape(bs, ts, NH, DH))
            outs["fs"].append(lf_s)
            outs["sp"].append(_unpair_states(s_p))
            outs["ss"].append(_unpair_states(s_s))
            last_p = proj[t - 1:lay.np_rows:t, :RW_IN]
            last_s = proj[lay.np_rows + (ts - 1) * bs:lay.ntok, :RW_IN]
            outs["shp"].append(_unperm_rwkv_cols(last_p))
            outs["shs"].append(_unperm_rwkv_cols(last_s))
        else:
            o = li // 2
            w_pool = pool_w[o].astype(BF16)
            p_scale = pool_scale[o].reshape(1, d)
            x_all, hl = _pool_prompt(lay, x_all, mod, li, g1, w_pool, p_scale)
            buf_t = jnp.swapaxes(state_pool[o], 0, 1)
            x_all, h_s = _pool_sample(lay, x_all, buf_t, mod, li, g1, w_pool, p_scale)
            outs["pp"].append(hl[:, HALO - POOL_BUF:, :])
            h_s_bt = jnp.swapaxes(h_s.reshape(ts, bs, d), 0, 1)
            outs["ps"].append(jnp.concatenate([state_pool[o], h_s_bt], axis=1)[:, -POOL_BUF:])
            x_all = _moe(lay, x_all, mod, li, g2, moe_router_w[o], moe_router_b[o],
                         o, moe_w_gate, moe_w_up, moe_w_down)

    y_p = x_all[:lay.np_rows].reshape(bp, t, d)
    y_s = jnp.swapaxes(x_all[lay.np_rows:lay.ntok].reshape(ts, bs, d), 0, 1)
    st = lambda k: jnp.stack(outs[k])
    return (y_p, y_s, st("kp"), st("ks"), st("vp"), st("vs"), st("fp"), st("fs"), st("sp"), st("ss"),
            st("shp"), st("shs"), st("pp"), st("ps"))
```

```python
import functools
import math

import jax
import jax.numpy as jnp
from jax import lax
from jax.experimental import pallas as pl
from jax.experimental.pallas import tpu as pltpu

F32 = jnp.float32
BF16 = jnp.bfloat16

NH = 8
DH = 64
HW = NH * DH
N_PAIR = NH // 2
LANE = 128
LORA_W = 64
LORA_A = 64
LORA_G = 128
RW_IN = 3 * HW + LORA_W + LORA_A + LORA_G
F_PAD = 256
PW = RW_IN + 3 * HW + F_PAD
COL_Q = RW_IN
COL_K = RW_IN + HW
COL_V = RW_IN + 2 * HW
COL_F = RW_IN + 3 * HW
PACK_W = 8 * HW
POOL_WINDOWS = (2, 4, 8, 16)
POOL_BUF = 15
HALO = 16
N_EXPERTS = 8
RMS_EPS = 1e-6
GN_EPS = 64e-5
NEG_INF = -1e30
PAGE = 128
VMEM_LIMIT = 56 * 1024 * 1024


def _cparams(*sem):
    return pltpu.CompilerParams(dimension_semantics=sem, vmem_limit_bytes=VMEM_LIMIT)


def _bdot(a, b):
    return jnp.dot(a.astype(BF16), b.astype(BF16), preferred_element_type=F32)


def _nt(a, b):
    return lax.dot_general(a.astype(BF16), b.astype(BF16), (((1,), (1,)), ((), ())),
                           preferred_element_type=F32)


def _tn(a, b):
    return lax.dot_general(a.astype(BF16), b.astype(BF16), (((0,), (0,)), ((), ())),
                           preferred_element_type=F32)


def _split(x, terms):
    out = []
    for _ in range(terms - 1):
        h = x.astype(BF16)
        out.append(h)
        x = x - h.astype(F32)
    out.append(x.astype(BF16))
    return out


def _dot_exact_lhs(a01, x, terms=3):
    a = a01.astype(BF16)
    acc = None
    for t in _split(x, terms):
        d = jnp.dot(a, t, preferred_element_type=F32)
        acc = d if acc is None else acc + d
    return acc


def _dot_exact_rhs(x, b01, terms=2):
    b = b01.astype(BF16)
    acc = None
    for t in _split(x, terms):
        d = jnp.dot(t, b, preferred_element_type=F32)
        acc = d if acc is None else acc + d
    return acc


def _dot3(a, b):
    ah, al = _split(a, 2)
    bh, bl = _split(b, 2)
    return (jnp.dot(ah, bh, preferred_element_type=F32) + jnp.dot(ah, bl, preferred_element_type=F32)
            + jnp.dot(al, bh, preferred_element_type=F32))


def _sigmoid(x):
    return 1.0 / (1.0 + jnp.exp(-x))


def _softplus(x):
    return jnp.maximum(x, 0.0) + jnp.log(1.0 + jnp.exp(-jnp.abs(x)))


def _iota(shape, dim):
    return lax.broadcasted_iota(jnp.int32, shape, dim)


def _seg_ones(n, seg):
    return (_iota((n, n), 0) // seg == _iota((n, n), 1) // seg).astype(BF16)


class _Layout:
    def __init__(self, bp, t, bs, ts):
        self.bp, self.t, self.bs, self.ts = bp, t, bs, ts
        self.np_rows = bp * t
        self.ms = bs * ts
        self.tm = min(1024, t)
        assert t % self.tm == 0 and self.tm % bs == 0 and self.ms <= self.tm
        assert self.np_rows % self.ms == 0 and bs % 8 == 0
        self.mp = self.np_rows + self.tm
        self.ntok = self.np_rows + self.ms
        self.bp_pad = 8 * ((bp + 7) // 8)
        self.mc = self.bp_pad + bs


def _tile_rows(lay, tm):
    assert lay.tm % tm == 0 and tm % lay.bs == 0
    return lay.mp // tm, lay.np_rows // tm, lay.t // tm


def _mod_params(lay, tm, i, refs):
    _, n_pt, tpb = _tile_rows(lay, tm)
    is_prompt = i < n_pt
    b = jnp.minimum(i // tpb, lay.bp - 1)
    out = []
    for r in refs:
        p_row = r[pl.ds(b, 1), :]
        s_blk = r[lay.bp_pad:lay.bp_pad + lay.bs, :]
        s_rows = jnp.concatenate([s_blk] * (tm // lay.bs), axis=0)
        out.append(jnp.where(is_prompt, p_row, s_rows))
    return out


def _rms(x, g):
    return x * lax.rsqrt(jnp.mean(x * x, axis=-1, keepdims=True) + RMS_EPS) * g


def _mod_spec(lay, li, k, nidx):
    if nidx == 1:
        return pl.BlockSpec((None, lay.mc, D_MODEL), lambda i: (li, 0, k))
    return pl.BlockSpec((None, lay.mc, D_MODEL), lambda i, j: (li, 0, k))


D_MODEL = 1024


def _ada_kernel(c_ref, w_ref, b_ref, o_ref):
    c = c_ref[...]
    s = c * _sigmoid(c)
    o_ref[...] = _dot3(s, w_ref[...]) + b_ref[...]


def _ada(c_all, ada_w, ada_b):
    nl, d, n = ada_w.shape
    mc = c_all.shape[0]
    tn = 512
    return pl.pallas_call(
        _ada_kernel,
        grid=(nl, n // tn),
        in_specs=[pl.BlockSpec((mc, d), lambda l, j: (0, 0)),
                  pl.BlockSpec((None, d, tn), lambda l, j: (l, 0, j)),
                  pl.BlockSpec((None, 1, tn), lambda l, j: (l, 0, j))],
        out_specs=pl.BlockSpec((None, mc, tn), lambda l, j: (l, 0, j)),
        out_shape=jax.ShapeDtypeStruct((nl, mc, n), F32),
        compiler_params=_cparams("arbitrary", "arbitrary"),
        name="ada",
    )(c_all, ada_w, ada_b.reshape(nl, 1, n))


def _inproj_kernel(x_ref, g_ref, sh_ref, sc_ref, w_ref, qkg_ref, fb_ref, o_ref, *, lay, tm):
    i = pl.program_id(0)
    sh, sc = _mod_params(lay, tm, i, (sh_ref, sc_ref))
    h = (_rms(x_ref[...], g_ref[...]) * (1.0 + sc) + sh).astype(BF16)
    ch = 256
    seg = _seg_ones(ch, DH)
    for c in range(PW // ch):
        c0 = c * ch
        acc = jnp.dot(h, w_ref[:, c0:c0 + ch], preferred_element_type=F32)
        if COL_Q <= c0 < COL_V:
            ss = _dot_exact_rhs(acc * acc, seg)
            acc = acc * lax.rsqrt(ss * (1.0 / DH) + RMS_EPS) * qkg_ref[:, c0 - COL_Q:c0 - COL_Q + ch]
        elif c0 >= COL_F:
            z = acc + fb_ref[...]
            acc = -_softplus(-z)
        o_ref[:, c0:c0 + ch] = acc


def _inproj(lay, x_all, mod, li, g, w_cat, qk_gain, f_bias):
    tm = min(512, lay.tm)
    nt, _, _ = _tile_rows(lay, tm)
    d = D_MODEL
    return pl.pallas_call(
        functools.partial(_inproj_kernel, lay=lay, tm=tm),
        grid=(nt,),
        in_specs=[pl.BlockSpec((tm, d), lambda i: (i, 0)),
                  pl.BlockSpec((1, d), lambda i: (0, 0)),
                  _mod_spec(lay, li, 0, 1), _mod_spec(lay, li, 1, 1),
                  pl.BlockSpec((d, PW), lambda i: (0, 0)),
                  pl.BlockSpec((1, 2 * HW), lambda i: (0, 0)),
                  pl.BlockSpec((1, F_PAD), lambda i: (0, 0))],
        out_specs=pl.BlockSpec((tm, PW), lambda i: (i, 0)),
        out_shape=jax.ShapeDtypeStruct((lay.mp, PW), F32),
        compiler_params=_cparams("arbitrary"),
        name="inproj",
    )(x_all, g, mod, mod, w_cat, qk_gain, f_bias)


def _cumf_kernel(lf_ref, fq_ref, carry_ref, *, tc):
    c = pl.program_id(1)

    @pl.when(c == 0)
    def _():
        carry_ref[...] = jnp.zeros_like(carry_ref)

    tri = _iota((tc, tc), 0) >= _iota((tc, tc), 1)
    f = _dot_exact_lhs(tri, lf_ref[...]) + carry_ref[...]
    carry_ref[...] = f[tc - 1:tc, :]
    expand = _iota((LANE, HW), 0) == _iota((LANE, HW), 1) // DH
    fq_ref[...] = _dot_exact_rhs(f, expand, terms=3)


def _cumf(lay, proj):
    tc = min(256, lay.t)
    npc = lay.t // tc
    return pl.pallas_call(
        functools.partial(_cumf_kernel, tc=tc),
        grid=(lay.bp, npc),
        in_specs=[pl.BlockSpec((tc, LANE), lambda b, c: (b * npc + c, COL_F // LANE))],
        out_specs=pl.BlockSpec((tc, HW), lambda b, c: (b * npc + c, 0)),
        out_shape=jax.ShapeDtypeStruct((lay.np_rows, HW), F32),
        scratch_shapes=[pltpu.VMEM((1, LANE), F32)],
        compiler_params=_cparams("arbitrary", "arbitrary"),
        name="cumf",
    )(proj)


N_BIAS = 3


def _bias_lanes(x, spare0, terms, ones_first):
    lane = _iota((1, LANE), 1)
    t_off, o_off = (N_BIAS, 0) if ones_first else (0, N_BIAS)
    for n, tval in enumerate(terms):
        x = jnp.where(lane == spare0 + t_off + n, tval, x)
    return jnp.where((lane >= spare0 + o_off) & (lane < spare0 + o_off + N_BIAS), 1.0, x)


def _split_f32(x):
    out = []
    for _ in range(N_BIAS - 1):
        h = x.astype(BF16).astype(F32)
        out.append(h)
        x = x - h
    out.append(x)
    return out


def _fox_prompt_kernel(q_ref, k_ref, v_ref, fq_ref, fs_ref, o_ref, kb_ref, vt_ref, *, tq, t):
    qi = pl.program_id(2)
    log2e = 1.4426950408889634
    lane_hi = _iota((1, LANE), 1) >= DH
    spare = (DH, 0)

    @pl.when(qi == 0)
    def _():
        for c0 in range(0, t, tq):
            kc = k_ref[c0:c0 + tq, :]
            fs = fs_ref[c0:c0 + tq, :] * log2e
            for hh in range(2):
                neg = [-x for x in _split_f32(fs[:, hh * DH:hh * DH + 1])]
                kb_ref[hh, c0:c0 + tq, :] = _bias_lanes(kc, spare[hh], neg, ones_first=False).astype(BF16)
            vt_ref[:, c0:c0 + tq] = v_ref[c0:c0 + tq, :].T.astype(BF16)

    q = q_ref[...] * (DH ** -0.5 * log2e)
    fq = fq_ref[...] * log2e
    qh = []
    for hh in range(2):
        qm = jnp.where(lane_hi, q, 0.0) if hh else jnp.where(lane_hi, 0.0, q)
        qh.append(_bias_lanes(qm, spare[hh], _split_f32(fq[:, hh * DH:hh * DH + 1]), ones_first=True).astype(BF16))
    key_le_query = _iota((tq, tq), 0) <= _iota((tq, tq), 1)

    def step(kj, carry, diagonal):
        k0 = pl.multiple_of(kj * tq, tq)
        out = []
        for hh in range(2):
            m, l, acc = carry[hh]
            st = lax.dot_general(kb_ref[hh, pl.ds(k0, tq), :], qh[hh], (((1,), (1,)), ((), ())),
                                 preferred_element_type=F32)
            if diagonal:
                st = jnp.where(key_le_query, st, NEG_INF)
            m_new = jnp.maximum(m, jnp.max(st, axis=0, keepdims=True))
            p = jnp.exp2(st - m_new)
            alpha = jnp.exp2(m - m_new)
            l = alpha * l + jnp.sum(p, axis=0, keepdims=True)
            acc = alpha * acc + jnp.dot(vt_ref[hh * DH:(hh + 1) * DH, pl.ds(k0, tq)], p.astype(BF16),
                                        preferred_element_type=F32)
            out.append((m_new, l, acc))
        return tuple(out)

    init = (jnp.full((1, tq), NEG_INF, F32), jnp.zeros((1, tq), F32), jnp.zeros((DH, tq), F32))
    carry = lax.fori_loop(0, qi, lambda kj, c: step(kj, c, False), (init, init))
    (_, l0, acc0), (_, l1, acc1) = step(qi, carry, True)
    o_ref[...] = jnp.concatenate([acc0 / l0, acc1 / l1], axis=0).T


def _fox_prompt(lay, proj, fq):
    tq = min(512, lay.t)
    nq = lay.t // tq
    blk = lambda col: pl.BlockSpec((tq, LANE), lambda b, p, i: (b * nq + i, col // LANE + p))
    seq = lambda col: pl.BlockSpec((lay.t, LANE), lambda b, p, i: (b, col // LANE + p))
    return pl.pallas_call(
        functools.partial(_fox_prompt_kernel, tq=tq, t=lay.t),
        grid=(lay.bp, N_PAIR, nq),
        in_specs=[blk(COL_Q), seq(COL_K), seq(COL_V),
                  pl.BlockSpec((tq, LANE), lambda b, p, i: (b * nq + i, p)),
                  pl.BlockSpec((lay.t, LANE), lambda b, p, i: (b, p))],
        out_specs=pl.BlockSpec((tq, LANE), lambda b, p, i: (b * nq + i, p)),
        out_shape=jax.ShapeDtypeStruct((lay.np_rows, HW), F32),
        scratch_shapes=[pltpu.VMEM((2, lay.t, LANE), BF16), pltpu.VMEM((LANE, lay.t), BF16)],
        compiler_params=_cparams("arbitrary", "arbitrary", "arbitrary"),
        name="fox_prompt",
    )(proj, proj, proj, fq, fq)


def _fox_sample_kernel(pt_ref, qx_ref, knt_ref, vn_ref, lfnt_ref, *rest, n_pages, ts):
    kt_refs = rest[:n_pages]
    vt_refs = rest[n_pages:2 * n_pages]
    lft_refs = rest[2 * n_pages:3 * n_pages]
    o_ref = rest[3 * n_pages]
    s_ref = rest[3 * n_pages + 1]
    nr = ts * NH
    tpad = vn_ref.shape[0]
    qx = qx_ref[...].astype(BF16)
    per_q = lambda a: jnp.concatenate([a] * ts, axis=0)
    row_q = _iota((nr, 1), 0) // NH

    cum_new = per_q(_dot_exact_rhs(lfnt_ref[...], _iota((tpad, tpad), 0) <= _iota((tpad, tpad), 1), terms=3))
    t_new = _iota((nr, tpad), 1)
    fn = jnp.sum(jnp.where(t_new == row_q, cum_new, 0.0), axis=1, keepdims=True)
    s_new = jnp.dot(qx, knt_ref[...].astype(BF16), preferred_element_type=F32) + fn - cum_new
    s_new = jnp.where((t_new <= row_q) & (t_new < ts), s_new, NEG_INF)

    later = _iota((PAGE, PAGE), 0) > _iota((PAGE, PAGE), 1)
    carry = jnp.zeros((nr, 1), F32)
    m = jnp.max(s_new, axis=1, keepdims=True)
    for j in range(n_pages - 1, -1, -1):
        lft = lft_refs[j][...]
        suffix = per_q(_dot_exact_rhs(lft, later, terms=3)) + carry
        carry = carry + per_q(jnp.sum(lft, axis=1, keepdims=True))
        s = jnp.dot(qx, kt_refs[j][...].astype(BF16), preferred_element_type=F32) + fn + suffix
        s_ref[:, j * PAGE:(j + 1) * PAGE] = s
        m = jnp.maximum(m, jnp.max(s, axis=1, keepdims=True))

    p_new = jnp.exp(s_new - m)
    p_all = jnp.exp(s_ref[...] - m)
    inv = 1.0 / (jnp.sum(p_new, axis=1, keepdims=True) + jnp.sum(p_all, axis=1, keepdims=True))
    acc = _bdot(p_new * inv, vn_ref[...])
    p_all = (p_all * inv).astype(BF16)
    for j in range(n_pages):
        acc = acc + _nt(p_all[:, j * PAGE:(j + 1) * PAGE], vt_refs[j][...])
    own = _iota((NH, HW), 0) == _iota((NH, HW), 1) // DH
    rows = [jnp.sum(jnp.where(own, acc[q * NH:(q + 1) * NH, :], 0.0), axis=0, keepdims=True)
            for q in range(ts)]
    o_ref[...] = jnp.concatenate(rows, axis=0)


def _fox_sample(lay, e, page_table, qx, knt, v_new, lfnt, cache_kt, cache_vt, cache_lft):
    bs, ts = lay.bs, lay.ts
    n_pages = page_table.shape[1]
    nr = ts * NH
    tpad = v_new.shape[1]
    page = lambda j: pl.BlockSpec((None, None, HW, PAGE), lambda b, pt, j=j: (e, pt[b * n_pages + j], 0, 0))
    lfpage = lambda j: pl.BlockSpec((None, None, NH, PAGE), lambda b, pt, j=j: (e, pt[b * n_pages + j], 0, 0))
    in_specs = ([pl.BlockSpec((None, nr, HW), lambda b, pt: (b, 0, 0)),
                 pl.BlockSpec((None, HW, tpad), lambda b, pt: (b, 0, 0)),
                 pl.BlockSpec((None, tpad, HW), lambda b, pt: (b, 0, 0)),
                 pl.BlockSpec((None, NH, tpad), lambda b, pt: (b, 0, 0))]
                + [page(j) for j in range(n_pages)] + [page(j) for j in range(n_pages)]
                + [lfpage(j) for j in range(n_pages)])
    grid_spec = pltpu.PrefetchScalarGridSpec(
        num_scalar_prefetch=1, grid=(bs,), in_specs=in_specs,
        out_specs=pl.BlockSpec((None, ts, HW), lambda b, pt: (b, 0, 0)),
        scratch_shapes=[pltpu.VMEM((nr, n_pages * PAGE), F32)])
    return pl.pallas_call(
        functools.partial(_fox_sample_kernel, n_pages=n_pages, ts=ts),
        grid_spec=grid_spec,
        out_shape=jax.ShapeDtypeStruct((bs, ts, HW), F32),
        compiler_params=_cparams("arbitrary"),
        name="fox_sample",
    )(page_table.reshape(-1), qx, knt, v_new, lfnt,
      *([cache_kt] * n_pages), *([cache_vt] * n_pages), *([cache_lft] * n_pages))


def _rwkv_pre_kernel(p_ref, st_ref, mu_ref, w0_ref, w2_ref, a0_ref, a2_ref, g2_ref, kk_ref, ka_ref, rk_ref,
                     o_ref, carry_ref, *, lay, tm):
    i = pl.program_id(0)
    _, n_pt, tpb = _tile_rows(lay, tm)
    p = p_ref[...]

    @pl.when(i == 0)
    def _():
        carry_ref[...] = jnp.zeros_like(carry_ref)

    first = (i % tpb) == 0
    prev_row = jnp.where(first, 0.0, carry_ref[...])
    rolled = pltpu.roll(p, 1, 0)
    sh_prompt = jnp.where(_iota((tm, 1), 0) == 0, prev_row, rolled)
    sh_sample = jnp.concatenate([st_ref[...], p[:tm - lay.bs, :]], axis=0)
    shifted = jnp.where(i < n_pt, sh_prompt, sh_sample)
    carry_ref[...] = p[tm - 1:tm, :]
    xm = p + (shifted - p) * mu_ref[...]
    r = xm[:, 0:HW]
    k = xm[:, HW:2 * HW]
    v = xm[:, 2 * HW:3 * HW]
    la = xm[:, 3 * HW:3 * HW + LANE]
    gl = xm[:, 3 * HW + LANE:RW_IN]
    w = -_softplus(-(w0_ref[...] + _bdot(jnp.tanh(la), w2_ref[...]))) - 0.5
    a = _sigmoid(a0_ref[...] + _bdot(la, a2_ref[...]))
    g = _bdot(_sigmoid(gl), g2_ref[...])
    seg = _seg_ones(HW, DH)
    kkr = k * kk_ref[...]
    kk = kkr * lax.rsqrt(jnp.maximum(_dot_exact_rhs(kkr * kkr, seg), 1e-24))
    kmod = k * (1.0 + (a - 1.0) * ka_ref[...])
    bonus = _dot_exact_rhs(r * kmod * rk_ref[...], seg) * v
    for n, val in enumerate((r, -jnp.exp(w), kmod, v, kk, kk * a, g, bonus)):
        o_ref[:, n * HW:(n + 1) * HW] = val


def _rwkv_pre(lay, proj, state_shift, rw):
    tm = min(512, lay.tm)
    nt, _, _ = _tile_rows(lay, tm)
    row = lambda n: pl.BlockSpec((1, n), lambda i: (0, 0))
    full = lambda a, b: pl.BlockSpec((a, b), lambda i: (0, 0))
    return pl.pallas_call(
        functools.partial(_rwkv_pre_kernel, lay=lay, tm=tm),
        grid=(nt,),
        in_specs=[pl.BlockSpec((tm, RW_IN), lambda i: (i, 0)),
                  full(lay.bs, RW_IN), row(RW_IN), row(HW), full(LANE, HW), row(HW), full(LANE, HW),
                  full(LORA_G, HW), row(HW), row(HW), row(HW)],
        out_specs=pl.BlockSpec((tm, PACK_W), lambda i: (i, 0)),
        out_shape=jax.ShapeDtypeStruct((lay.mp, PACK_W), F32),
        scratch_shapes=[pltpu.VMEM((1, RW_IN), F32)],
        compiler_params=_cparams("arbitrary"),
        name="rwkv_pre",
    )(proj, state_shift, rw["mu"], rw["w0"], rw["w2"], rw["a0"], rw["a2"], rw["g2"], rw["k_k"], rw["k_a"],
      rw["r_k"])


def _rwkv_scan_kernel(*refs, c, nbb):
    x_refs = refs[:nbb]
    s0_ref, lnw_ref, lnb_ref, o_ref, so_ref, s_ref = refs[nbb:]
    ci = pl.program_id(1)

    @pl.when(ci == 0)
    def _():
        s_ref[...] = s0_ref[...]

    lane_hi = _iota((1, LANE), 1) >= DH
    tri_incl = _iota((c, c), 0) >= _iota((c, c), 1)
    rr = _iota((2 * c, 2 * c), 0)
    cc = _iota((2 * c, 2 * c), 1)
    strict = (rr % c) > (cc % c)
    incl = (rr % c) >= (cc % c)
    eye = (rr == cc).astype(F32)
    seg = _seg_ones(LANE, DH)

    def stack(x):
        return jnp.concatenate([jnp.where(lane_hi, 0.0, x), jnp.where(lane_hi, x, 0.0)], axis=0)

    chains = [(k, p) for k in range(nbb) for p in range(N_PAIR)]
    each = lambda f, *lists: [f(*args) for args in zip(*lists)]
    col = lambda n: [x_refs[k][:, n * HW + p * LANE:n * HW + (p + 1) * LANE] for k, p in chains]
    r, lw, km, v, kk, b, g, bonus = (col(n) for n in range(8))
    sbd = [s_ref[k, p] for k, p in chains]
    cum = each(lambda x: _dot_exact_lhs(tri_incl, x), lw)
    pin = each(jnp.exp, cum)
    pinv = each(lambda x: jnp.exp(-x), cum)
    a_s = each(lambda kk_, cm, lw_: stack(-kk_ * jnp.exp(cm - lw_)), kk, cum, lw)
    b_s = each(lambda x, q: stack(x * q), b, pinv)
    k_s = each(lambda x, q: stack(x * q), km, pinv)
    r_s = each(lambda x, q: stack(x * q), r, pin)
    v_s = each(stack, v)
    if (2 * c) % LANE == 0:
        ar = each(lambda x, y: jnp.concatenate([x, y], axis=0), a_s, r_s)
        bk = each(lambda x, y: jnp.concatenate([x, y], axis=0), b_s, k_s)
        keep = jnp.concatenate([jnp.concatenate([strict, strict], axis=1),
                                jnp.concatenate([incl, incl], axis=1)], axis=0)
        n_all = each(lambda x, y: jnp.where(keep, _nt(x, y), 0.0), ar, bk)
        n_ab = each(lambda n_: n_[:2 * c, :2 * c], n_all)
        n_rb = each(lambda n_: n_[2 * c:, :2 * c], n_all)
        zy = each(lambda x, s_, n_, v_: _nt(x, s_) + _bdot(n_[:, 2 * c:], v_), ar, sbd, n_all, v_s)
        z = each(lambda x: x[:2 * c], zy)
        ys0 = each(lambda x: x[2 * c:], zy)
    else:
        bk = None
        n_ab = each(lambda x, y: jnp.where(strict, _nt(x, y), 0.0), a_s, b_s)
        n_ak = each(lambda x, y: jnp.where(strict, _nt(x, y), 0.0), a_s, k_s)
        n_rb = each(lambda x, y: jnp.where(incl, _nt(x, y), 0.0), r_s, b_s)
        n_rk = each(lambda x, y: jnp.where(incl, _nt(x, y), 0.0), r_s, k_s)
        z = each(lambda a_, s_, n_, v_: _nt(a_, s_) + _bdot(n_, v_), a_s, sbd, n_ak, v_s)
        ys0 = each(lambda r_, s_, n_, v_: _nt(r_, s_) + _bdot(n_, v_), r_s, sbd, n_rk, v_s)
    inv = each(lambda n_: eye + n_, n_ab)
    pw = n_ab
    for _ in range(max(0, math.ceil(math.log2(c)) - 1)):
        pw = each(lambda x: _bdot(x, x), pw)
        inv = each(lambda x, q: x + _bdot(x, q), inv, pw)
    u = each(_bdot, inv, z)
    ys = each(lambda y0, n_, u_: y0 + _bdot(n_, u_), ys0, n_rb, u)
    if bk is not None:
        s_new = each(lambda s_, u_, v_, bk_, p_: (s_ + _tn(jnp.concatenate([u_, v_], axis=0), bk_)) * p_[c - 1:c, :],
                     sbd, u, v_s, bk, pin)
    else:
        s_new = each(lambda s_, u_, b_, v_, k_, p_: (s_ + _tn(u_, b_) + _tn(v_, k_)) * p_[c - 1:c, :],
                     sbd, u, b_s, v_s, k_s, pin)
    y = each(lambda x: x[:c] + x[c:], ys)
    mean = each(lambda x: _dot_exact_rhs(x, seg, terms=1) * (1.0 / DH), y)
    dlt = each(lambda x, m_: x - m_, y, mean)
    var = each(lambda x: _dot_exact_rhs(x * x, seg, terms=1) * (1.0 / DH), dlt)
    for i, (k, p) in enumerate(chains):
        s_ref[k, p] = s_new[i]
        ln = slice(p * LANE, (p + 1) * LANE)
        yn = dlt[i] * lax.rsqrt(var[i] + GN_EPS) * lnw_ref[:, ln] + lnb_ref[:, ln]
        o_ref[k, :, ln] = (yn + bonus[i]) * g[i]

    @pl.when(ci == pl.num_programs(1) - 1)
    def _():
        so_ref[...] = s_ref[...]


def _rwkv_scan(packed, s0_bd, ln_w, ln_b, nb, tn, c, nbb):
    nc = tn // c
    assert nb % nbb == 0
    xspec = lambda k: pl.BlockSpec((c, PACK_W), lambda gb, ci, k=k: ((gb * nbb + k) * nc + ci, 0))
    return pl.pallas_call(
        functools.partial(_rwkv_scan_kernel, c=c, nbb=nbb),
        grid=(nb // nbb, nc),
        in_specs=[xspec(k) for k in range(nbb)] + [
            pl.BlockSpec((nbb, N_PAIR, LANE, LANE), lambda gb, ci: (gb, 0, 0, 0)),
            pl.BlockSpec((1, HW), lambda gb, ci: (0, 0)),
            pl.BlockSpec((1, HW), lambda gb, ci: (0, 0))],
        out_specs=[pl.BlockSpec((nbb, c, HW), lambda gb, ci: (gb, ci, 0)),
                   pl.BlockSpec((nbb, N_PAIR, LANE, LANE), lambda gb, ci: (gb, 0, 0, 0))],
        out_shape=[jax.ShapeDtypeStruct((nb, tn, HW), F32),
                   jax.ShapeDtypeStruct((nb, N_PAIR, LANE, LANE), F32)],
        scratch_shapes=[pltpu.VMEM((nbb, N_PAIR, LANE, LANE), F32)],
        compiler_params=_cparams("arbitrary", "arbitrary"),
        name="rwkv_scan",
    )(*([packed] * nbb), s0_bd, ln_w, ln_b)


def _pair_states(s):
    b = s.shape[0]
    s = s.reshape(b, N_PAIR, 2, DH, DH)
    z = jnp.zeros_like(s[:, :, 0])
    top = jnp.concatenate([s[:, :, 0], z], axis=-1)
    bot = jnp.concatenate([z, s[:, :, 1]], axis=-1)
    return jnp.concatenate([top, bot], axis=-2)


def _unpair_states(sbd):
    b = sbd.shape[0]
    return jnp.stack([sbd[:, :, :DH, :DH], sbd[:, :, DH:, DH:]], axis=2).reshape(b, NH, DH, DH)


def _outproj_kernel(x_ref, ofp_ref, orp_ref, ofs_ref, ors_ref, gt_ref, w_ref, o_ref, *, lay, tm):
    i = pl.program_id(0)
    _, n_pt, _ = _tile_rows(lay, tm)
    (gate,) = _mod_params(lay, tm, i, (gt_ref,))
    is_prompt = i < n_pt
    o_fox = jnp.where(is_prompt, ofp_ref[...], ofs_ref[...])
    o_rwkv = jnp.where(is_prompt, orp_ref[...], ors_ref[...])
    y = (jnp.dot(o_fox.astype(BF16), w_ref[0:HW, :], preferred_element_type=F32)
         + jnp.dot(o_rwkv.astype(BF16), w_ref[HW:2 * HW, :], preferred_element_type=F32))
    o_ref[...] = x_ref[...] + gate * y


def _outproj(lay, x_all, o_fox_p, o_rw_p, o_fox_s, o_rw_s, mod, li, w_out):
    tm = lay.tm
    nt, n_pt, _ = _tile_rows(lay, tm)
    d = D_MODEL
    prow = lambda i: (jnp.minimum(i, n_pt - 1), 0)
    return pl.pallas_call(
        functools.partial(_outproj_kernel, lay=lay, tm=tm),
        grid=(nt,),
        in_specs=[pl.BlockSpec((tm, d), lambda i: (i, 0)),
                  pl.BlockSpec((tm, HW), prow),
                  pl.BlockSpec((tm, HW), prow),
                  pl.BlockSpec((tm, HW), lambda i: (0, 0)),
                  pl.BlockSpec((tm, HW), lambda i: (0, 0)),
                  _mod_spec(lay, li, 2, 1),
                  pl.BlockSpec((2 * HW, d), lambda i: (0, 0))],
        out_specs=pl.BlockSpec((tm, d), lambda i: (i, 0)),
        out_shape=jax.ShapeDtypeStruct((lay.mp, d), F32),
        compiler_params=_cparams("arbitrary"),
        name="outproj",
    )(x_all, o_fox_p, o_rw_p, o_fox_s, o_rw_s, mod, w_out)


def _ffn_kernel(x_ref, g_ref, sh_ref, sc_ref, gt_ref, wg_ref, wu_ref, wd_ref, o_ref, h_ref, acc_ref, *, lay, tm):
    i = pl.program_id(0)
    f = pl.program_id(1)

    @pl.when(f == 0)
    def _():
        sh, sc = _mod_params(lay, tm, i, (sh_ref, sc_ref))
        h_ref[...] = (_rms(x_ref[...], g_ref[...]) * (1.0 + sc) + sh).astype(BF16)
        acc_ref[...] = jnp.zeros_like(acc_ref)

    h = h_ref[...]
    a = jnp.dot(h, wg_ref[...].astype(BF16), preferred_element_type=F32)
    u = jnp.dot(h, wu_ref[...].astype(BF16), preferred_element_type=F32)
    hid = (a * _sigmoid(a) * u).astype(BF16)
    acc_ref[...] += jnp.dot(hid, wd_ref[...].astype(BF16), preferred_element_type=F32)

    @pl.when(f == pl.num_programs(1) - 1)
    def _():
        (gate,) = _mod_params(lay, tm, i, (gt_ref,))
        o_ref[...] = x_ref[...] + gate * acc_ref[...]


def _ffn(lay, x_all, mod, li, g, e, wg, wu, wd):
    tm = lay.tm
    nt, _, _ = _tile_rows(lay, tm)
    d = D_MODEL
    dff = wg.shape[2]
    tf = 256
    return pl.pallas_call(
        functools.partial(_ffn_kernel, lay=lay, tm=tm),
        grid=(nt, dff // tf),
        in_specs=[pl.BlockSpec((tm, d), lambda i, f: (i, 0)),
                  pl.BlockSpec((1, d), lambda i, f: (0, 0)),
                  _mod_spec(lay, li, 3, 2), _mod_spec(lay, li, 4, 2), _mod_spec(lay, li, 5, 2),
                  pl.BlockSpec((None, d, tf), lambda i, f: (e, 0, f)),
                  pl.BlockSpec((None, d, tf), lambda i, f: (e, 0, f)),
                  pl.BlockSpec((None, tf, d), lambda i, f: (e, f, 0))],
        out_specs=pl.BlockSpec((tm, d), lambda i, f: (i, 0)),
        out_shape=jax.ShapeDtypeStruct((lay.mp, d), F32),
        scratch_shapes=[pltpu.VMEM((tm, d), BF16), pltpu.VMEM((tm, d), F32)],
        compiler_params=_cparams("arbitrary", "arbitrary"),
        name="ffn",
    )(x_all, g, mod, mod, mod, wg, wu, wd)


def _pool_mix(pooled_minus_h, w_ref, ps_ref):
    pc = D_MODEL // len(POOL_WINDOWS)
    ys = [jnp.dot(pooled_minus_h[gi].astype(BF16), w_ref[gi], preferred_element_type=F32)
          for gi in range(len(POOL_WINDOWS))]
    return jnp.concatenate(ys, axis=-1) * ps_ref[...]


def _pool_prompt_kernel(x_ref, g_ref, sh_ref, sc_ref, gt_ref, w_ref, ps_ref, o_ref, hl_ref, carry_ref, *, lay, tm):
    i = pl.program_id(0)
    _, _, tpb = _tile_rows(lay, tm)
    sh, sc, gate = _mod_params(lay, tm, i, (sh_ref, sc_ref, gt_ref))
    h = _rms(x_ref[...], g_ref[...]) * (1.0 + sc) + sh

    @pl.when(i == 0)
    def _():
        carry_ref[...] = jnp.zeros_like(carry_ref)

    first = (i % tpb) == 0
    halo = jnp.where(first, 0.0, carry_ref[...])
    carry_ref[...] = h[tm - HALO:, :]
    hl_ref[...] = h[tm - HALO:, :]
    ext = jnp.concatenate([halo, h], axis=0)
    pos = (i % tpb) * tm + _iota((tm, 1), 0)
    pc = D_MODEL // len(POOL_WINDOWS)
    zs = []
    for gi, win in enumerate(POOL_WINDOWS):
        s = ext[:, gi * pc:(gi + 1) * pc]
        step = 1
        while step < win:
            s = s + pltpu.roll(s, step, 0)
            step *= 2
        cnt = jnp.minimum(pos + 1, win).astype(F32)
        zs.append(s[HALO:, :] / cnt - h[:, gi * pc:(gi + 1) * pc])
    o_ref[...] = x_ref[...] + gate * _pool_mix(zs, w_ref, ps_ref)


def _pool_prompt(lay, x_all, mod, li, g, w_pool, pool_scale):
    tm = lay.tm
    _, n_pt, tpb = _tile_rows(lay, tm)
    d = D_MODEL
    pc = d // len(POOL_WINDOWS)
    return pl.pallas_call(
        functools.partial(_pool_prompt_kernel, lay=lay, tm=tm),
        grid=(n_pt,),
        in_specs=[pl.BlockSpec((tm, d), lambda i: (i, 0)),
                  pl.BlockSpec((1, d), lambda i: (0, 0)),
                  _mod_spec(lay, li, 0, 1), _mod_spec(lay, li, 1, 1), _mod_spec(lay, li, 2, 1),
                  pl.BlockSpec((len(POOL_WINDOWS), pc, pc), lambda i: (0, 0, 0)),
                  pl.BlockSpec((1, d), lambda i: (0, 0))],
        out_specs=[pl.BlockSpec((tm, d), lambda i: (i, 0)),
                   pl.BlockSpec((None, HALO, d), lambda i: (i // tpb, 0, 0))],
        out_shape=[jax.ShapeDtypeStruct((lay.mp, d), F32),
                   jax.ShapeDtypeStruct((lay.bp, HALO, d), F32)],
        scratch_shapes=[pltpu.VMEM((HALO, d), F32)],
        input_output_aliases={0: 0},
        compiler_params=_cparams("arbitrary"),
        name="pool_prompt",
    )(x_all, g, mod, mod, mod, w_pool, pool_scale)


def _pool_sample_kernel(x_ref, buf_ref, g_ref, sh_ref, sc_ref, gt_ref, w_ref, ps_ref, o_ref, h_ref, *, lay):
    bs, ts = lay.bs, lay.ts
    sl = slice(lay.bp_pad, lay.bp_pad + bs)
    sh, sc, gate = sh_ref[sl, :], sc_ref[sl, :], gt_ref[sl, :]
    pc = D_MODEL // len(POOL_WINDOWS)
    hs = []
    for t in range(ts):
        hs.append(_rms(x_ref[t * bs:(t + 1) * bs, :], g_ref[...]) * (1.0 + sc) + sh)
        h_ref[t * bs:(t + 1) * bs, :] = hs[t]
    ext = [buf_ref[e] for e in range(POOL_BUF)] + hs
    for t in range(ts):
        zs = []
        for gi, win in enumerate(POOL_WINDOWS):
            cs = slice(gi * pc, (gi + 1) * pc)
            s = ext[POOL_BUF + t][:, cs]
            for j in range(1, win):
                s = s + ext[POOL_BUF + t - j][:, cs]
            zs.append(s * (1.0 / win) - hs[t][:, cs])
        o_ref[t * bs:(t + 1) * bs, :] = x_ref[t * bs:(t + 1) * bs, :] + gate * _pool_mix(zs, w_ref, ps_ref)


def _pool_sample(lay, x_all, buf_t, mod, li, g, w_pool, pool_scale):
    d = D_MODEL
    pc = d // len(POOL_WINDOWS)
    sblk = lay.np_rows // lay.ms
    mspec = lambda k: pl.BlockSpec((None, lay.mc, d), lambda i: (li, 0, k))
    return pl.pallas_call(
        functools.partial(_pool_sample_kernel, lay=lay),
        grid=(1,),
        in_specs=[pl.BlockSpec((lay.ms, d), lambda i: (sblk, 0)),
                  pl.BlockSpec((POOL_BUF, lay.bs, d), lambda i: (0, 0, 0)),
                  pl.BlockSpec((1, d), lambda i: (0, 0)),
                  mspec(0), mspec(1), mspec(2),
                  pl.BlockSpec((len(POOL_WINDOWS), pc, pc), lambda i: (0, 0, 0)),
                  pl.BlockSpec((1, d), lambda i: (0, 0))],
        out_specs=[pl.BlockSpec((lay.ms, d), lambda i: (sblk, 0)),
                   pl.BlockSpec((lay.ms, d), lambda i: (0, 0))],
        out_shape=[jax.ShapeDtypeStruct((lay.mp, d), F32),
                   jax.ShapeDtypeStruct((lay.ms, d), F32)],
        input_output_aliases={0: 0},
        compiler_params=_cparams("arbitrary"),
        name="pool_sample",
    )(x_all, buf_t, g, mod, mod, mod, w_pool, pool_scale)


def _router_kernel(x_ref, g_ref, sh_ref, sc_ref, rw_ref, rb_ref, h_ref, meta_ref, *, lay, tm):
    i = pl.program_id(0)
    sh, sc = _mod_params(lay, tm, i, (sh_ref, sc_ref))
    h = _rms(x_ref[...], g_ref[...]) * (1.0 + sc) + sh
    h_ref[...] = h
    lane = _iota((tm, LANE), 1)
    logits = jnp.where(lane < N_EXPERTS, _dot3(h, rw_ref[...]) + rb_ref[...], NEG_INF)
    m1 = jnp.max(logits, axis=-1, keepdims=True)
    i1 = jnp.min(jnp.where(logits == m1, lane, LANE), axis=-1, keepdims=True)
    rest = jnp.where(lane == i1, NEG_INF, logits)
    m2 = jnp.max(rest, axis=-1, keepdims=True)
    i2 = jnp.min(jnp.where(rest == m2, lane, LANE), axis=-1, keepdims=True)
    e2 = jnp.exp(m2 - m1)
    g1 = 1.0 / (1.0 + e2)
    g2 = e2 / (1.0 + e2)
    meta = jnp.where(lane == 0, i1.astype(F32), 0.0)
    meta = jnp.where(lane == 1, i2.astype(F32), meta)
    meta = jnp.where(lane == 2, g1, meta)
    meta_ref[...] = jnp.where(lane == 3, g2, meta)


def _router(lay, x_all, mod, li, g, rw_pad, rb_pad):
    tm = min(512, lay.tm)
    nt, _, _ = _tile_rows(lay, tm)
    d = D_MODEL
    return pl.pallas_call(
        functools.partial(_router_kernel, lay=lay, tm=tm),
        grid=(nt,),
        in_specs=[pl.BlockSpec((tm, d), lambda i: (i, 0)),
                  pl.BlockSpec((1, d), lambda i: (0, 0)),
                  _mod_spec(lay, li, 3, 1), _mod_spec(lay, li, 4, 1),
                  pl.BlockSpec((d, LANE), lambda i: (0, 0)),
                  pl.BlockSpec((1, LANE), lambda i: (0, 0))],
        out_specs=[pl.BlockSpec((tm, d), lambda i: (i, 0)),
                   pl.BlockSpec((tm, LANE), lambda i: (i, 0))],
        out_shape=[jax.ShapeDtypeStruct((lay.mp, d), F32),
                   jax.ShapeDtypeStruct((lay.mp, LANE), F32)],
        compiler_params=_cparams("arbitrary"),
        name="router",
    )(x_all, g, mod, mod, rw_pad, rb_pad)


def _dispatch_kernel(dest_ref, h_ref, z_ref, xg_ref, sem, *, tm):
    del z_ref
    base = pl.program_id(0) * tm

    def row_copies(r):
        return tuple(pltpu.make_async_copy(h_ref.at[pl.ds(r, 1)], xg_ref.at[pl.ds(dest_ref[2 * (base + r) + k], 1)],
                                           sem.at[k]) for k in range(2))

    def start(r, c):
        for cp in row_copies(r):
            cp.start()
        return c

    def wait(r, c):
        for cp in row_copies(r):
            cp.wait()
        return c

    lax.fori_loop(0, tm, start, 0, unroll=32)
    lax.fori_loop(0, tm, wait, 0, unroll=8)


def _dispatch_rows(lay, h2, dest, n_slots):
    tm = lay.tm
    d = h2.shape[1]
    grid_spec = pltpu.PrefetchScalarGridSpec(
        num_scalar_prefetch=1, grid=(lay.mp // tm,),
        in_specs=[pl.BlockSpec((tm, d), lambda i, dest: (i, 0)),
                  pl.BlockSpec(memory_space=pl.ANY)],
        out_specs=pl.BlockSpec(memory_space=pl.ANY),
        scratch_shapes=[pltpu.SemaphoreType.DMA((2,))])
    return pl.pallas_call(
        functools.partial(_dispatch_kernel, tm=tm), grid_spec=grid_spec,
        out_shape=jax.ShapeDtypeStruct((n_slots, d), h2.dtype),
        input_output_aliases={2: 0},
        compiler_params=pltpu.CompilerParams(dimension_semantics=("arbitrary",), vmem_limit_bytes=VMEM_LIMIT,
                                             disable_bounds_checks=True),
        name="dispatch_rows",
    )(dest, h2, jnp.zeros((n_slots, d), h2.dtype))


def _expert_kernel(te_ref, nu_ref, x_ref, wg_ref, wu_ref, wd_ref, o_ref, h_ref, acc_ref):
    i = pl.program_id(0)
    f = pl.program_id(1)
    used = i < nu_ref[0]

    @pl.when(jnp.logical_and(used, f == 0))
    def _():
        h_ref[...] = x_ref[...].astype(BF16)
        acc_ref[...] = jnp.zeros_like(acc_ref)

    @pl.when(used)
    def _():
        h = h_ref[...]
        a = jnp.dot(h, wg_ref[...].astype(BF16), preferred_element_type=F32)
        u = jnp.dot(h, wu_ref[...].astype(BF16), preferred_element_type=F32)
        hid = (a * _sigmoid(a) * u).astype(BF16)
        acc_ref[...] += jnp.dot(hid, wd_ref[...].astype(BF16), preferred_element_type=F32)

    @pl.when(f == pl.num_programs(1) - 1)
    def _():
        o_ref[...] = jnp.where(used, acc_ref[...], 0.0)


def _experts(xg, mg, tile_expert, n_used, o, wg, wu, wd, tme):
    d = xg.shape[1]
    dff = wg.shape[3]
    tf = 512
    nf = dff // tf

    def widx(i, f, te, nu):
        return jnp.where(i < nu[0], f, nf - 1)

    grid_spec = pltpu.PrefetchScalarGridSpec(
        num_scalar_prefetch=2, grid=(mg // tme, nf),
        in_specs=[pl.BlockSpec((tme, d), lambda i, f, te, nu: (jnp.minimum(i, nu[0] - 1), 0)),
                  pl.BlockSpec((None, None, d, tf), lambda i, f, te, nu: (o, te[i], 0, widx(i, f, te, nu))),
                  pl.BlockSpec((None, None, d, tf), lambda i, f, te, nu: (o, te[i], 0, widx(i, f, te, nu))),
                  pl.BlockSpec((None, None, tf, d), lambda i, f, te, nu: (o, te[i], widx(i, f, te, nu), 0))],
        out_specs=pl.BlockSpec((tme, d), lambda i, f, te, nu: (i, 0)),
        scratch_shapes=[pltpu.VMEM((tme, d), BF16), pltpu.VMEM((tme, d), F32)])
    return pl.pallas_call(
        _expert_kernel, grid_spec=grid_spec,
        out_shape=jax.ShapeDtypeStruct((mg, d), F32),
        compiler_params=_cparams("arbitrary", "arbitrary"),
        name="experts",
    )(tile_expert, n_used, xg, wg, wu, wd)


def _combine_kernel(d1_ref, d2_ref, x_ref, meta_ref, gt_ref, yg_ref, o_ref, y1_ref, y2_ref, sem, *, lay, tm):
    i = pl.program_id(0)
    base = i * tm

    def row_copies(r):
        return (pltpu.make_async_copy(yg_ref.at[pl.ds(d1_ref[base + r], 1)], y1_ref.at[pl.ds(r, 1)], sem.at[0]),
                pltpu.make_async_copy(yg_ref.at[pl.ds(d2_ref[base + r], 1)], y2_ref.at[pl.ds(r, 1)], sem.at[1]))

    def start(r, c):
        for cp in row_copies(r):
            cp.start()
        return c

    def wait(r, c):
        for cp in row_copies(r):
            cp.wait()
        return c

    lax.fori_loop(0, tm, start, 0, unroll=32)
    (gate,) = _mod_params(lay, tm, i, (gt_ref,))
    meta = meta_ref[...]
    lax.fori_loop(0, tm, wait, 0, unroll=8)
    y = meta[:, 2:3] * y1_ref[...] + meta[:, 3:4] * y2_ref[...]
    o_ref[...] = x_ref[...] + gate * y


def _combine(lay, x_all, yg, dest2, meta, mod, li):
    tm = lay.tm
    nt, _, _ = _tile_rows(lay, tm)
    d = D_MODEL
    grid_spec = pltpu.PrefetchScalarGridSpec(
        num_scalar_prefetch=2, grid=(nt,),
        in_specs=[pl.BlockSpec((tm, d), lambda i, d1, d2: (i, 0)),
                  pl.BlockSpec((tm, LANE), lambda i, d1, d2: (i, 0)),
                  pl.BlockSpec((None, lay.mc, d), lambda i, d1, d2: (li, 0, 5)),
                  pl.BlockSpec(memory_space=pl.ANY)],
        out_specs=pl.BlockSpec((tm, d), lambda i, d1, d2: (i, 0)),
        scratch_shapes=[pltpu.VMEM((tm, d), F32), pltpu.VMEM((tm, d), F32), pltpu.SemaphoreType.DMA((2,))])
    return pl.pallas_call(
        functools.partial(_combine_kernel, lay=lay, tm=tm),
        grid_spec=grid_spec,
        out_shape=jax.ShapeDtypeStruct((lay.mp, d), F32),
        compiler_params=pltpu.CompilerParams(dimension_semantics=("arbitrary",), vmem_limit_bytes=VMEM_LIMIT,
                                             disable_bounds_checks=True),
        name="combine",
    )(dest2[:, 0], dest2[:, 1], x_all, meta, mod, yg)


def _moe(lay, x_all, mod, li, g, router_w, router_b, o, wg, wu, wd):
    d = D_MODEL
    rw_pad = jnp.pad(router_w, ((0, 0), (0, LANE - N_EXPERTS)))
    rb_pad = jnp.pad(router_b, (0, LANE - N_EXPERTS)).reshape(1, LANE)
    h2, meta = _router(lay, x_all, mod, li, g, rw_pad, rb_pad)

    ntok = lay.ntok
    tme = lay.tm
    ex = meta[:ntok, 0:2].astype(jnp.int32).reshape(-1)
    onehot = (ex[:, None] == jnp.arange(N_EXPERTS)[None, :]).astype(jnp.int32)
    rank = jnp.sum((jnp.cumsum(onehot, axis=0) - onehot) * onehot, axis=1)
    counts = jnp.sum(onehot, axis=0)
    tiles = (counts + tme - 1) // tme
    tile_end = jnp.cumsum(tiles)
    offs = (tile_end - tiles) * tme
    dest = offs[ex] + rank
    n_tiles = (2 * ntok + N_EXPERTS * (tme - 1)) // tme + 1
    mg = n_tiles * tme
    tile_expert = jnp.minimum(jnp.sum(jnp.arange(n_tiles)[:, None] >= tile_end[None, :], axis=1),
                              N_EXPERTS - 1).astype(jnp.int32)
    n_used = tile_end[-1:].astype(jnp.int32)
    last_e = tile_expert[jnp.maximum(n_used[0] - 1, 0)]
    tile_expert = jnp.where(jnp.arange(n_tiles) < n_used[0], tile_expert, last_e)

    spare = mg + jnp.arange(2 * (lay.mp - ntok), dtype=jnp.int32)
    xg = _dispatch_rows(lay, h2, jnp.concatenate([dest.astype(jnp.int32), spare]), mg + spare.shape[0])
    yg = _experts(xg, mg, tile_expert, n_used, o, wg, wu, wd, tme)
    dest2 = jnp.pad(dest.reshape(ntok, 2), ((0, lay.mp - ntok), (0, 0)))
    return _combine(lay, x_all, yg, dest2, meta, mod, li)


def _perm_rwkv_cols(a):
    r, wl, k, v, al, gl = jnp.split(a, [HW, HW + LORA_W, 2 * HW + LORA_W, 3 * HW + LORA_W, 3 * HW + LORA_W + LORA_A],
                                    axis=-1)
    return jnp.concatenate([r, k, v, wl, al, gl], axis=-1)


def _unperm_rwkv_cols(a):
    r, k, v, wl, al, gl = jnp.split(a, [HW, 2 * HW, 3 * HW, 3 * HW + LORA_W, 3 * HW + LORA_W + LORA_A], axis=-1)
    return jnp.concatenate([r, wl, k, v, al, gl], axis=-1)


def kernel(x_prompt, x_sample, c_prompt, c_sample, cache_fox_k, cache_fox_v, cache_fox_logf, page_table, state_rwkv, state_rwkv_shift, state_pool, norm1_g, norm2_g, ada_w, ada_b, w_in, w_out, fox_q_gain, fox_k_gain, fox_f_bias, rwkv_mu, rwkv_w0, rwkv_w2, rwkv_a0, rwkv_a2, rwkv_g2, rwkv_k_k, rwkv_k_a, rwkv_r_k, rwkv_ln_w, rwkv_ln_b, ffn_w_gate, ffn_w_up, ffn_w_down, pool_w, pool_scale, moe_router_w, moe_router_b, moe_w_gate, moe_w_up, moe_w_down):
    bp, t, d = x_prompt.shape
    bs, ts, _ = x_sample.shape
    assert d == D_MODEL
    lay = _Layout(bp, t, bs, ts)
    depth = ada_w.shape[0]
    n_phys = cache_fox_k.shape[1]
    fox_in = 3 * HW + NH

    xs_tb = jnp.swapaxes(x_sample, 0, 1).reshape(lay.ms, d)
    x_all = jnp.concatenate([x_prompt.reshape(lay.np_rows, d), xs_tb, jnp.zeros((lay.mp - lay.ntok, d), F32)], axis=0)
    c_all = jnp.concatenate([c_prompt, jnp.zeros((lay.bp_pad - bp, d), F32), c_sample], axis=0)
    mod = _ada(c_all, ada_w, ada_b)

    slot_minor = lambda c: jnp.transpose(c, (0, 1, 3, 4, 2)).reshape(c.shape[0], n_phys, HW, PAGE)
    cache_kt = slot_minor(cache_fox_k)
    cache_vt = slot_minor(cache_fox_v)
    cache_lft = jnp.swapaxes(cache_fox_logf, 2, 3)
    head_of_lane = jnp.arange(HW) // DH
    qmask = (jnp.arange(NH)[:, None] == head_of_lane[None, :]).astype(F32)
    tpad = 8

    outs = {k: [] for k in ("kp", "ks", "vp", "vs", "fp", "fs", "sp", "ss", "shp", "shs", "pp", "ps")}
    for li in range(depth):
        g1 = norm1_g[li].reshape(1, d)
        g2 = norm2_g[li].reshape(1, d)
        if li % 2 == 0:
            e = li // 2
            wi = w_in[e]
            w_cat = jnp.concatenate(
                [_perm_rwkv_cols(wi[:, fox_in:]), wi[:, :3 * HW],
                 jnp.pad(wi[:, 3 * HW:fox_in], ((0, 0), (0, F_PAD - NH)))], axis=1).astype(BF16)
            qk_gain = jnp.concatenate([jnp.tile(fox_q_gain[e], NH), jnp.tile(fox_k_gain[e], NH)]).reshape(1, 2 * HW)
            f_bias = jnp.pad(fox_f_bias[e], (0, F_PAD - NH)).reshape(1, F_PAD)
            proj = _inproj(lay, x_all, mod, li, g1, w_cat, qk_gain, f_bias)

            o_fox_p = _fox_prompt(lay, proj, _cumf(lay, proj))

            ps_rows = proj[lay.np_rows:lay.ntok].reshape(ts, bs, PW).swapaxes(0, 1)
            q_s = ps_rows[..., COL_Q:COL_K] * (DH ** -0.5)
            k_s = ps_rows[..., COL_K:COL_V]
            v_s = ps_rows[..., COL_V:COL_F]
            lf_s = ps_rows[..., COL_F:COL_F + NH]
            qx = (q_s[:, :, None, :] * qmask[None, None, :, :]).reshape(bs, ts * NH, HW)
            padt = lambda a: jnp.pad(a, ((0, 0), (0, tpad - ts), (0, 0)))
            o_fox_s = _fox_sample(lay, e, page_table, qx, jnp.swapaxes(padt(k_s), 1, 2), padt(v_s),
                                  jnp.swapaxes(padt(lf_s), 1, 2), cache_kt, cache_vt, cache_lft)
            sample_tile = lambda a: jnp.pad(a, ((0, lay.tm - lay.ms), (0, 0)))
            o_fox_s = sample_tile(jnp.swapaxes(o_fox_s, 0, 1).reshape(lay.ms, HW))

            pad_lora = lambda w_, top: jnp.pad(w_, ((0, LANE - w_.shape[0]), (0, 0)) if top else
                                               ((LANE - w_.shape[0], 0), (0, 0))).astype(BF16)
            rw = dict(mu=_perm_rwkv_cols(rwkv_mu[e]).reshape(1, RW_IN), w0=rwkv_w0[e].reshape(1, HW),
                      w2=pad_lora(rwkv_w2[e], True), a0=rwkv_a0[e].reshape(1, HW), a2=pad_lora(rwkv_a2[e], False),
                      g2=rwkv_g2[e].astype(BF16), k_k=rwkv_k_k[e].reshape(1, HW), k_a=rwkv_k_a[e].reshape(1, HW),
                      r_k=rwkv_r_k[e].reshape(1, HW))
            packed = _rwkv_pre(lay, proj, _perm_rwkv_cols(state_rwkv_shift[e]), rw)
            ln_w = rwkv_ln_w[e].reshape(1, HW)
            ln_b = rwkv_ln_b[e].reshape(1, HW)
            cp = min(64, t)
            o_rw_p, s_p = _rwkv_scan(packed, jnp.zeros((bp, N_PAIR, LANE, LANE), F32), ln_w, ln_b,
                                     bp, t, cp, 2 if bp % 2 == 0 else 1)
            pk_s = packed[lay.np_rows:lay.ntok].reshape(ts, bs, PACK_W).swapaxes(0, 1)
            pk_s = jnp.pad(pk_s, ((0, 0), (0, tpad - ts), (0, 0))).reshape(bs * tpad, PACK_W)
            o_rw_s, s_s = _rwkv_scan(pk_s, _pair_states(state_rwkv[e]), ln_w, ln_b, bs, tpad, tpad,
                                     4 if bs % 4 == 0 else 1)
            o_rw_s = sample_tile(o_rw_s[:, :ts].swapaxes(0, 1).reshape(lay.ms, HW))

            x_all = _outproj(lay, x_all, o_fox_p, o_rw_p.reshape(lay.np_rows, HW), o_fox_s, o_rw_s, mod, li,
                             w_out[e].astype(BF16))
            x_all = _ffn(lay, x_all, mod, li, g2, e, ffn_w_gate, ffn_w_up, ffn_w_down)

            outs["kp"].append(proj[:lay.np_rows, COL_K:COL_V].reshape(bp, t, NH, DH))
            outs["vp"].append(proj[:lay.np_rows, COL_V:COL_F].reshape(bp, t, NH, DH))
            outs["fp"].append(proj[:lay.np_rows, COL_F:COL_F + NH].reshape(bp, t, NH))
            outs["ks"].append(k_s.reshape(bs, ts, NH, DH))
            outs["vs"].append(v_s.reshape(bs, ts, NH, DH))
            outs["fs"].append(lf_s)
            outs["sp"].append(_unpair_states(s_p))
            outs["ss"].append(_unpair_states(s_s))
            last_p = proj[t - 1:lay.np_rows:t, :RW_IN]
            last_s = proj[lay.np_rows + (ts - 1) * bs:lay.ntok, :RW_IN]
            outs["shp"].append(_unperm_rwkv_cols(last_p))
            outs["shs"].append(_unperm_rwkv_cols(last_s))
        else:
            o = li // 2
            w_pool = pool_w[o].astype(BF16)
            p_scale = pool_scale[o].reshape(1, d)
            x_all, hl = _pool_prompt(lay, x_all, mod, li, g1, w_pool, p_scale)
            buf_t = jnp.swapaxes(state_pool[o], 0, 1)
            x_all, h_s = _pool_sample(lay, x_all, buf_t, mod, li, g1, w_pool, p_scale)
            outs["pp"].append(hl[:, HALO - POOL_BUF:, :])
            h_s_bt = jnp.swapaxes(h_s.reshape(ts, bs, d), 0, 1)
            outs["ps"].append(jnp.concatenate([state_pool[o], h_s_bt], axis=1)[:, -POOL_BUF:])
            x_all = _moe(lay, x_all, mod, li, g2, moe_router_w[o], moe_router_b[o],
                         o, moe_w_gate, moe_w_up, moe_w_down)

    y_p = x_all[:lay.np_rows].reshape(bp, t, d)
    y_s = jnp.swapaxes(x_all[lay.np_rows:lay.ntok].reshape(ts, bs, d), 0, 1)
    st = lambda k: jnp.stack(outs[k])
    return (y_p, y_s, st("kp"), st("ks"), st("vp"), st("vs"), st("fp"), st("fs"), st("sp"), st("ss"),
            st("shp"), st("shs"), st("pp"), st("ps"))
```
